```python
import math
import jax, jax.numpy as jnp
from jax import lax
import numpy as np

D_MODEL = 1024
BATCH = 8
SEQ = 4096
DEPTH = 2

PLE_DIM = 256
D_POOL = 512
POOL_WINDOWS = (2, 4, 8, 16)
N_POOL_GROUPS = len(POOL_WINDOWS)
POOL_GROUP = D_POOL // N_POOL_GROUPS
D_ATTN = D_MODEL - D_POOL
N_HEADS = 8
HEAD_DIM = D_ATTN // N_HEADS
MOBA_BLOCK = 256
MOBA_TOPK = 3
Q_CHUNK = 32
NUM_BUCKETS = 32
MAX_DISTANCE = 128
D_FF = 2816
CONV_WIDTH = 3
EPS = 1e-6
D_IN = D_POOL + 3 * D_ATTN

kernel_name = "hybrid_pool_moba_convffn_block"


def rms_norm(x, g):
    xf = x.astype(jnp.float32)
    y = xf * lax.rsqrt(jnp.mean(xf * xf, axis=-1, keepdims=True) + EPS)
    return (y * g.astype(jnp.float32)).astype(x.dtype)


def pool_mixer(u, w_pool, pool_scale):
    B, S, _ = u.shape
    uf = u.astype(jnp.float32)
    cpad = jnp.pad(jnp.cumsum(uf, axis=1), ((0, 0), (1, 0), (0, 0)))
    pos = jnp.arange(S)
    outs = []
    for g, w in enumerate(POOL_WINDOWS):
        c_g = cpad[:, :, g * POOL_GROUP:(g + 1) * POOL_GROUP]
        lagged = jnp.pad(c_g[:, :S + 1 - w], ((0, 0), (w - 1, 0), (0, 0)))
        count = jnp.minimum(pos + 1, w).astype(jnp.float32)[None, :, None]
        mean = (c_g[:, 1:] - lagged) / count
        outs.append(mean - uf[:, :, g * POOL_GROUP:(g + 1) * POOL_GROUP])
    pooled = jnp.stack(outs, axis=2)
    mixed = jnp.einsum('bsgc,gcd->bsgd', pooled, w_pool.astype(jnp.float32))
    return (mixed.reshape(B, S, D_POOL) * pool_scale.astype(jnp.float32)).astype(u.dtype)


def rel_bucket(dist):
    n = jnp.maximum(dist, 0)
    max_exact = NUM_BUCKETS // 2
    nf = jnp.maximum(n, 1).astype(jnp.float32)
    large = max_exact + (jnp.log(nf / max_exact) / math.log(MAX_DISTANCE / max_exact)
                         * (NUM_BUCKETS - max_exact)).astype(jnp.int32)
    large = jnp.minimum(large, NUM_BUCKETS - 1)
    return jnp.where(n < max_exact, n, large)


def moba_attention(q, k, v, rel_bias):
    B, S = q.shape[0], q.shape[1]
    nb = -(-S // MOBA_BLOCK)
    s_pad = nb * MOBA_BLOCK
    pad = ((0, 0), (0, s_pad - S), (0, 0), (0, 0))
    q, k, v = [jnp.pad(t, pad).transpose(0, 2, 1, 3) for t in (q, k, v)]
    k_blk = k.reshape(B, N_HEADS, nb, MOBA_BLOCK, HEAD_DIM)
    v_blk = v.reshape(B, N_HEADS, nb, MOBA_BLOCK, HEAD_DIM)
    k_mean = jnp.mean(k_blk.astype(jnp.float32), axis=3)
    gate = jnp.einsum('bhsd,bhnd->bhsn', q.astype(jnp.float32), k_mean)
    q_blk_idx = jnp.arange(s_pad) // MOBA_BLOCK
    past = jnp.arange(nb)[None, :] < q_blk_idx[:, None]
    gate = jnp.where(past, gate, -jnp.inf)
    n_sel = min(MOBA_TOPK, nb)
    _, sel = lax.top_k(gate, n_sel)
    sel_valid = sel < q_blk_idx[:, None]

    scale = HEAD_DIM ** -0.5
    bias_t = rel_bias.astype(jnp.float32).T
    b_idx = jnp.arange(B)[:, None, None, None]
    h_idx = jnp.arange(N_HEADS)[None, :, None, None]
    key_off = jnp.arange(MOBA_BLOCK)
    n_sel_keys = n_sel * MOBA_BLOCK

    def chunk(ci):
        start = ci * Q_CHUNK
        qc = lax.dynamic_slice_in_dim(q, start, Q_CHUNK, axis=2)
        sc = lax.dynamic_slice_in_dim(sel, start, Q_CHUNK, axis=2)
        vc = lax.dynamic_slice_in_dim(sel_valid, start, Q_CHUNK, axis=2)
        t = start + jnp.arange(Q_CHUNK)
        j = start // MOBA_BLOCK
        k_sel = k_blk[b_idx, h_idx, sc]
        v_sel = v_blk[b_idx, h_idx, sc]
        key_pos = sc[..., None] * MOBA_BLOCK + key_off
        bias_sel = bias_t[h_idx[..., None], rel_bucket(t[:, None, None] - key_pos)]
        l_sel = jnp.einsum('bhqd,bhqnkd->bhqnk', qc, k_sel).astype(jnp.float32) * scale + bias_sel
        l_sel = jnp.where(vc[..., None], l_sel, -jnp.inf)
        k_own = lax.dynamic_index_in_dim(k_blk, j, axis=2, keepdims=False)
        v_own = lax.dynamic_index_in_dim(v_blk, j, axis=2, keepdims=False)
        d_own = t[:, None] - (j * MOBA_BLOCK + key_off)[None, :]
        bias_own = bias_t[:, rel_bucket(d_own)]
        l_own = jnp.einsum('bhqd,bhkd->bhqk', qc, k_own).astype(jnp.float32) * scale + bias_own
        l_own = jnp.where(d_own >= 0, l_own, -jnp.inf)
        logits = jnp.concatenate([l_sel.reshape(B, N_HEADS, Q_CHUNK, n_sel_keys), l_own], axis=-1)
        probs = jax.nn.softmax(logits, axis=-1).astype(v.dtype)
        p_sel = probs[..., :n_sel_keys].reshape(B, N_HEADS, Q_CHUNK, n_sel, MOBA_BLOCK)
        p_own = probs[..., n_sel_keys:]
        return (jnp.einsum('bhqnk,bhqnkd->bhqd', p_sel, v_sel)
                + jnp.einsum('bhqk,bhkd->bhqd', p_own, v_own))

    outs = lax.map(chunk, jnp.arange(s_pad // Q_CHUNK))
    out = outs.transpose(1, 0, 3, 2, 4).reshape(B, s_pad, N_HEADS, HEAD_DIM)[:, :S]
    return out.reshape(B, S, D_ATTN)


def conv_ffn(h, w_up, conv_w, conv_b, w_down):
    S = h.shape[1]
    gate, val = jnp.split(h @ w_up, 2, axis=-1)
    gp = jnp.pad(gate, ((0, 0), (CONV_WIDTH - 1, 0), (0, 0)))
    conv = conv_b + gp[:, 0:S] * conv_w[0]
    for kk in range(1, CONV_WIDTH):
        conv = conv + gp[:, kk:kk + S] * conv_w[kk]
    return (jax.nn.gelu(conv, approximate=True) * val) @ w_down


def setup_inputs(seed: int = 0) -> dict:
    key = jax.random.key(seed)
    ks = jax.random.split(key, 20)
    f32 = jnp.float32
    nrm = lambda k, shape, s: jax.random.normal(k, shape, f32) * s
    gain = lambda k: 1.0 + nrm(k, (DEPTH, D_MODEL), 0.05)
    return {
        "x": nrm(ks[0], (BATCH, SEQ, D_MODEL), 1.0),
        "p": nrm(ks[1], (DEPTH, BATCH, SEQ, PLE_DIM), 1.0),
        "rel_bias": nrm(ks[2], (NUM_BUCKETS, N_HEADS), 0.5),
        "g_mix_pre": gain(ks[3]),
        "g_mix_post": gain(ks[4]),
        "g_ffn_pre": gain(ks[5]),
        "g_ffn_post": gain(ks[6]),
        "w_in": nrm(ks[7], (DEPTH, D_MODEL, D_IN), D_MODEL ** -0.5),
        "w_pool": nrm(ks[8], (DEPTH, N_POOL_GROUPS, POOL_GROUP, POOL_GROUP), POOL_GROUP ** -0.5),
        "pool_scale": 1.0 + nrm(ks[9], (DEPTH, D_POOL), 0.1),
        "w_out": nrm(ks[10], (DEPTH, D_MODEL, D_MODEL), D_MODEL ** -0.5),
        "w_up": nrm(ks[11], (DEPTH, D_MODEL, 2 * D_FF), D_MODEL ** -0.5),
        "conv_w": nrm(ks[12], (DEPTH, CONV_WIDTH, D_FF), CONV_WIDTH ** -0.5),
        "conv_b": nrm(ks[13], (DEPTH, D_FF), 0.02),
        "w_down": nrm(ks[14], (DEPTH, D_FF, D_MODEL), D_FF ** -0.5),
        "w_ple": nrm(ks[15], (DEPTH, PLE_DIM, D_MODEL), PLE_DIM ** -0.5),
        "w_ple_gate": nrm(ks[16], (DEPTH, D_MODEL, D_MODEL), D_MODEL ** -0.5),
    }


def reference(x, p, rel_bias, g_mix_pre, g_mix_post, g_ffn_pre, g_ffn_post, w_in, w_pool,
              pool_scale, w_out, w_up, conv_w, conv_b, w_down, w_ple, w_ple_gate):
    B, S, _ = x.shape
    for i in range(DEPTH):
        h = rms_norm(x, g_mix_pre[i])
        proj = h @ w_in[i]
        u, q, k, v = jnp.split(proj, [D_POOL, D_POOL + D_ATTN, D_POOL + 2 * D_ATTN], axis=-1)
        y_pool = pool_mixer(u, w_pool[i], pool_scale[i])
        y_attn = moba_attention(q.reshape(B, S, N_HEADS, HEAD_DIM),
                                k.reshape(B, S, N_HEADS, HEAD_DIM),
                                v.reshape(B, S, N_HEADS, HEAD_DIM), rel_bias)
        y = jnp.concatenate([y_pool, y_attn], axis=-1) @ w_out[i]
        x = x + rms_norm(y, g_mix_post[i])
        h = rms_norm(x, g_ffn_pre[i])
        x = x + rms_norm(conv_ffn(h, w_up[i], conv_w[i], conv_b[i], w_down[i]), g_ffn_post[i])
        x = x + (p[i] @ w_ple[i]) * jax.nn.sigmoid(x @ w_ple_gate[i])
    return x
```

```python
import functools
import math

import numpy as np
import jax
import jax.numpy as jnp
from jax import lax
from jax.experimental import pallas as pl
from jax.experimental.pallas import tpu as pltpu

D_MODEL = 1024
PLE_DIM = 256
D_POOL = 512
POOL_WINDOWS = (2, 4, 8, 16)
POOL_GROUP = D_POOL // len(POOL_WINDOWS)
D_ATTN = D_MODEL - D_POOL
N_HEADS = 8
HEAD_DIM = D_ATTN // N_HEADS
MOBA_BLOCK = 256
MOBA_TOPK = 3
NUM_BUCKETS = 32
MAX_DISTANCE = 128
D_FF = 2816
CONV_WIDTH = 3
EPS = 1e-6

HEADS_PER_STEP = 2
HEAD_PAIRS = N_HEADS // HEADS_PER_STEP
POOL_HALO = 16
CONV_HALO = 8
FF_CHUNK = 256
N_FF_CHUNKS = D_FF // FF_CHUNK
ROW_TILE = 512
MASKED = -1e30
VMEM_LIMIT = 56 * 1024 * 1024

_BF16 = jnp.bfloat16
_F32 = jnp.float32


def _rms(x, g):
    return x * lax.rsqrt(jnp.mean(x * x, axis=-1, keepdims=True) + EPS) * g


def _dot(a, b):
    return jnp.dot(a, b, preferred_element_type=_F32)


def _bucket_tables():
    key = np.arange(MOBA_BLOCK)[:, None]
    qry = np.arange(MOBA_BLOCK)[None, :]

    def bucket(dist):
        n = np.maximum(dist, 0)
        max_exact = NUM_BUCKETS // 2
        nf = np.maximum(n, 1).astype(np.float32)
        large = max_exact + (np.log(nf / max_exact) / math.log(MAX_DISTANCE / max_exact)
                             * (NUM_BUCKETS - max_exact)).astype(np.int32)
        large = np.minimum(large, NUM_BUCKETS - 1)
        return np.where(n < max_exact, n, large).astype(np.int32)

    d_own = qry - key
    own = np.where(d_own >= 0, bucket(d_own), -1).astype(np.int32)
    adj = bucket(d_own + MOBA_BLOCK)
    return own, adj


def _bias_table_kernel(rb_ref, own_idx_ref, adj_idx_ref, own_ref, adj_ref):
    h = pl.program_id(0)
    far = rb_ref[NUM_BUCKETS - 1, h]
    own_idx = own_idx_ref[...]
    adj_idx = adj_idx_ref[...]
    own = jnp.zeros(own_idx.shape, _F32)
    adj = jnp.zeros(adj_idx.shape, _F32)
    for b in range(NUM_BUCKETS):
        val = rb_ref[b, h] - far
        own = jnp.where(own_idx == b, val, own)
        adj = jnp.where(adj_idx == b, val, adj)
    own_ref[0] = jnp.where(own_idx < 0, MASKED, own)
    adj_ref[0] = adj


def _bias_tables(rel_bias):
    own_idx, adj_idx = _bucket_tables()
    blk = pl.BlockSpec((MOBA_BLOCK, MOBA_BLOCK), lambda h: (0, 0))
    out = pl.BlockSpec((1, MOBA_BLOCK, MOBA_BLOCK), lambda h: (h, 0, 0))
    shape = jax.ShapeDtypeStruct((N_HEADS, MOBA_BLOCK, MOBA_BLOCK), _F32)
    return pl.pallas_call(
        _bias_table_kernel,
        grid=(N_HEADS,),
        in_specs=[pl.BlockSpec(memory_space=pltpu.SMEM), blk, blk],
        out_specs=[out, out],
        out_shape=[shape, shape],
        name="bias_tables",
    )(rel_bias.astype(_F32), jnp.asarray(own_idx), jnp.asarray(adj_idx))


def _proj_pool_kernel(x_ref, g_ref, wu_ref, wk_ref, wqv_ref, wpool_ref, pscale_ref,
                      k_ref, qv_ref, ypool_ref, ext_ref, *, tiles_per_seq):
    i = pl.program_id(0)
    tm = x_ref.shape[0]
    h = _rms(x_ref[...], g_ref[...]).astype(_BF16)

    k_ref[...] = _dot(h, wk_ref[...]).astype(_BF16)

    qv = lax.dot_general(wqv_ref[...], h, (((1,), (1,)), ((), ())),
                         preferred_element_type=_F32)
    row = lax.broadcasted_iota(jnp.int32, qv.shape, 0)
    qv = jnp.where(row < D_ATTN, qv * (HEAD_DIM ** -0.5), qv).astype(_BF16)
    for c in range(tm // MOBA_BLOCK):
        qv_ref[c] = qv[:, c * MOBA_BLOCK:(c + 1) * MOBA_BLOCK]

    u = _dot(h, wu_ref[...])
    seq_tile = i % tiles_per_seq

    @pl.when(seq_tile == 0)
    def _():
        ext_ref[0:POOL_HALO, :] = jnp.zeros((POOL_HALO, D_POOL), _F32)

    ext_ref[POOL_HALO:POOL_HALO + tm, :] = u
    pos = seq_tile * tm + lax.broadcasted_iota(jnp.int32, (tm, 1), 0)
    for g, w in enumerate(POOL_WINDOWS):
        cols = slice(g * POOL_GROUP, (g + 1) * POOL_GROUP)
        u_g = u[:, cols]
        acc = u_g
        for lag in range(1, w):
            acc = acc + ext_ref[POOL_HALO - lag:POOL_HALO - lag + tm, cols]
        count = jnp.minimum(pos + 1, w).astype(_F32)
        pooled = (acc / count - u_g).astype(_BF16)
        mixed = _dot(pooled, wpool_ref[g]) * pscale_ref[:, cols]
        ypool_ref[:, cols] = mixed.astype(_BF16)
    ext_ref[0:POOL_HALO, :] = u[tm - POOL_HALO:, :]


def _proj_pool(x2d, g, wu, wk, wqv, wpool, pscale, *, seq_len):
    t = x2d.shape[0]
    tm = ROW_TILE
    const = lambda shape: pl.BlockSpec(shape, lambda i: (0,) * len(shape))
    return pl.pallas_call(
        functools.partial(_proj_pool_kernel, tiles_per_seq=seq_len // tm),
        grid=(t // tm,),
        in_specs=[
            pl.BlockSpec((tm, D_MODEL), lambda i: (i, 0)),
            const((1, D_MODEL)),
            const((D_MODEL, D_POOL)),
            const((D_MODEL, D_ATTN)),
            const((2 * D_ATTN, D_MODEL)),
            const((len(POOL_WINDOWS), POOL_GROUP, POOL_GROUP)),
            const((1, D_POOL)),
        ],
        out_specs=[
            pl.BlockSpec((tm, D_ATTN), lambda i: (i, 0)),
            pl.BlockSpec((tm // MOBA_BLOCK, 2 * D_ATTN, MOBA_BLOCK), lambda i: (i, 0, 0)),
            pl.BlockSpec((tm, D_POOL), lambda i: (i, 0)),
        ],
        out_shape=[
            jax.ShapeDtypeStruct((t, D_ATTN), _BF16),
            jax.ShapeDtypeStruct((t // MOBA_BLOCK, 2 * D_ATTN, MOBA_BLOCK), _BF16),
            jax.ShapeDtypeStruct((t, D_POOL), _BF16),
        ],
        scratch_shapes=[pltpu.VMEM((POOL_HALO + tm, D_POOL), _F32)],
        compiler_params=pltpu.CompilerParams(
            dimension_semantics=("arbitrary",), vmem_limit_bytes=VMEM_LIMIT),
        name="proj_pool",
    )(x2d, g, wu, wk, wqv, wpool, pscale)


def _select_blocks(gate, n_past):
    blk = lax.broadcasted_iota(jnp.int32, gate.shape, 0)
    n_blocks = gate.shape[0]
    g = jnp.where(blk < n_past, gate, -jnp.inf)
    sel = jnp.zeros(gate.shape, _F32)
    for _ in range(MOBA_TOPK):
        mx = jnp.max(g, axis=0, keepdims=True)
        is_max = jnp.logical_and(g == mx, mx > -jnp.inf)
        first = jnp.min(jnp.where(is_max, blk, n_blocks), axis=0, keepdims=True)
        pick = blk == first
        sel = jnp.where(pick, 1.0, sel)
        g = jnp.where(pick, -jnp.inf, g)
    return sel


def _moba_kernel(q_ref, k_ref, v_ref, own_ref, adj_ref, o_ref, kmean_ref, sel_ref):
    j = pl.program_id(2)

    @pl.when(j == 0)
    def _():
        kmean_ref[...] = jnp.sum(k_ref[...].astype(_F32), axis=1) * (1.0 / MOBA_BLOCK)

    q_t = q_ref[0]
    feat = lax.broadcasted_iota(jnp.int32, q_t.shape, 0)
    zero = jnp.zeros_like(q_t)
    q_heads = [jnp.where(feat < HEAD_DIM, q_t, zero), jnp.where(feat >= HEAD_DIM, q_t, zero)]

    kmean = kmean_ref[...].astype(_BF16)
    for a in range(HEADS_PER_STEP):
        sel_ref[a] = _select_blocks(_dot(kmean, q_heads[a]), j)

    def update(n, extra_bias, use_flag, state):
        kb = k_ref[n]
        new_state = []
        for a in range(HEADS_PER_STEP):
            m, l, acc = state[a]
            s = _dot(kb, q_heads[a])
            if extra_bias is not None:
                s = s + extra_bias[a]
            if use_flag:
                s = jnp.where(sel_ref[a, pl.ds(n, 1), :] > 0.0, s, MASKED)
            m_new = jnp.maximum(m, jnp.max(s, axis=0, keepdims=True))
            alpha = jnp.exp(m - m_new)
            p = jnp.exp(s - m_new)
            l = alpha * l + jnp.sum(p, axis=0, keepdims=True)
            v_t = v_ref[n, a * HEAD_DIM:(a + 1) * HEAD_DIM, :]
            acc = alpha * acc + _dot(v_t, p.astype(_BF16))
            new_state.append((m_new, l, acc))
        return tuple(new_state)

    tq = q_t.shape[1]
    init = tuple((jnp.full((1, tq), MASKED, _F32), jnp.zeros((1, tq), _F32),
                  jnp.zeros((HEAD_DIM, tq), _F32)) for _ in range(HEADS_PER_STEP))
    state = update(j, [own_ref[a] for a in range(HEADS_PER_STEP)], False, init)
    state = update(jnp.maximum(j - 1, 0), [adj_ref[a] for a in range(HEADS_PER_STEP)], True, state)
    state = lax.fori_loop(0, jnp.maximum(j - 1, 0),
                          lambda n, st: update(n, None, True, st), state)

    out_t = jnp.concatenate([acc / l for (_, l, acc) in state], axis=0)
    o_ref[...] = out_t.T.astype(_BF16)


def _moba(qv3, k3, own, adj, *, batch, seq_len):
    nb = seq_len // MOBA_BLOCK
    t = batch * seq_len
    pair = HEADS_PER_STEP * HEAD_DIM
    return pl.pallas_call(
        _moba_kernel,
        grid=(batch, HEAD_PAIRS, nb),
        in_specs=[
            pl.BlockSpec((1, pair, MOBA_BLOCK), lambda b, hp, j: (b * nb + j, hp, 0)),
            pl.BlockSpec((nb, MOBA_BLOCK, pair), lambda b, hp, j: (b, 0, hp)),
            pl.BlockSpec((nb, pair, MOBA_BLOCK), lambda b, hp, j: (b, HEAD_PAIRS + hp, 0)),
            pl.BlockSpec((HEADS_PER_STEP, MOBA_BLOCK, MOBA_BLOCK), lambda b, hp, j: (hp, 0, 0)),
            pl.BlockSpec((HEADS_PER_STEP, MOBA_BLOCK, MOBA_BLOCK), lambda b, hp, j: (hp, 0, 0)),
        ],
        out_specs=pl.BlockSpec((MOBA_BLOCK, pair), lambda b, hp, j: (b * nb + j, hp)),
        out_shape=jax.ShapeDtypeStruct((t, D_ATTN), _BF16),
        scratch_shapes=[pltpu.VMEM((nb, pair), _F32),
                        pltpu.VMEM((HEADS_PER_STEP, nb, MOBA_BLOCK), _F32)],
        compiler_params=pltpu.CompilerParams(
            dimension_semantics=("arbitrary", "arbitrary", "arbitrary"),
            vmem_limit_bytes=VMEM_LIMIT),
        name="moba_attention",
    )(qv3, k3, qv3, own, adj)


def _shift_rows(g, prev_rows, lag):
    rolled = pltpu.roll(g, lag, 0)
    rid = lax.broadcasted_iota(jnp.int32, prev_rows.shape, 0)
    head = jnp.where(rid < lag, pltpu.roll(prev_rows, lag, 0), rolled[:CONV_HALO])
    return jnp.concatenate([head, rolled[CONV_HALO:]], axis=0)


def _ffn_kernel(x_ref, p_ref, ypool_ref, yattn_ref, wo_pool_ref, wo_attn_ref, g_post_ref,
                g_pre_ref, wgate_ref, wval_ref, convw_ref, convb_ref, wdown_ref, g_ffn_post_ref,
                wple_ref, wplegate_ref, o_ref, carry_ref, acc_ref, *, tiles_per_seq):
    i = pl.program_id(0)
    tm = x_ref.shape[0]

    @pl.when(i % tiles_per_seq == 0)
    def _():
        carry_ref[...] = jnp.zeros(carry_ref.shape, _F32)

    y = _dot(ypool_ref[...], wo_pool_ref[...]) + _dot(yattn_ref[...], wo_attn_ref[...])
    x1 = x_ref[...] + _rms(y, g_post_ref[...])
    h = _rms(x1, g_pre_ref[...]).astype(_BF16)

    for c in range(N_FF_CHUNKS):
        cols = slice(c * FF_CHUNK, (c + 1) * FF_CHUNK)
        gate = _dot(h, wgate_ref[c])
        val = _dot(h, wval_ref[c])
        prev = carry_ref[:, cols]
        conv = convb_ref[:, cols] + _shift_rows(gate, prev, 2) * convw_ref[0:1, cols]
        conv = conv + _shift_rows(gate, prev, 1) * convw_ref[1:2, cols]
        conv = conv + gate * convw_ref[2:3, cols]
        carry_ref[:, cols] = gate[tm - CONV_HALO:, :]
        act = (jax.nn.gelu(conv, approximate=True) * val).astype(_BF16)
        down = _dot(act, wdown_ref[c])
        if c == 0:
            acc_ref[...] = down
        else:
            acc_ref[...] += down

    x2 = x1 + _rms(acc_ref[...], g_ffn_post_ref[...])
    ple = _dot(p_ref[...].astype(_BF16), wple_ref[...])
    gate_logit = _dot(x2.astype(_BF16), wplegate_ref[...])
    o_ref[...] = x2 + ple * (1.0 / (1.0 + jnp.exp(-gate_logit)))


def _ffn(x2d, p2d, ypool, yattn, wo_pool, wo_attn, g_post, g_pre, wgate, wval, convw, convb,
         wdown, g_ffn_post, wple, wplegate, *, seq_len):
    t = x2d.shape[0]
    tm = ROW_TILE
    rows = lambda width: pl.BlockSpec((tm, width), lambda i: (i, 0))
    const = lambda shape: pl.BlockSpec(shape, lambda i: (0,) * len(shape),
                                       pipeline_mode=pl.Buffered(1))
    return pl.pallas_call(
        functools.partial(_ffn_kernel, tiles_per_seq=seq_len // tm),
        grid=(t // tm,),
        in_specs=[
            rows(D_MODEL), rows(PLE_DIM), rows(D_POOL), rows(D_ATTN),
            const((D_POOL, D_MODEL)), const((D_ATTN, D_MODEL)), const((1, D_MODEL)),
            const((1, D_MODEL)),
            const((N_FF_CHUNKS, D_MODEL, FF_CHUNK)), const((N_FF_CHUNKS, D_MODEL, FF_CHUNK)),
            const((CONV_WIDTH, D_FF)), const((1, D_FF)),
            const((N_FF_CHUNKS, FF_CHUNK, D_MODEL)), const((1, D_MODEL)),
            const((PLE_DIM, D_MODEL)), const((D_MODEL, D_MODEL)),
        ],
        out_specs=rows(D_MODEL),
        out_shape=jax.ShapeDtypeStruct((t, D_MODEL), _F32),
        scratch_shapes=[pltpu.VMEM((CONV_HALO, D_FF), _F32), pltpu.VMEM((tm, D_MODEL), _F32)],
        compiler_params=pltpu.CompilerParams(
            dimension_semantics=("arbitrary",), vmem_limit_bytes=VMEM_LIMIT),
        name="ffn_block",
    )(x2d, p2d, ypool, yattn, wo_pool, wo_attn, g_post, g_pre, wgate, wval, convw, convb,
      wdown, g_ffn_post, wple, wplegate)


def _chunk_cols(w):
    return w.reshape(w.shape[0], N_FF_CHUNKS, FF_CHUNK).transpose(1, 0, 2)


def kernel(x, p, rel_bias, g_mix_pre, g_mix_post, g_ffn_pre, g_ffn_post, w_in, w_pool,
           pool_scale, w_out, w_up, conv_w, conv_b, w_down, w_ple, w_ple_gate):
    batch, seq_len, d_model = x.shape
    depth = w_in.shape[0]
    assert d_model == D_MODEL and seq_len % ROW_TILE == 0 and ROW_TILE % MOBA_BLOCK == 0
    t = batch * seq_len
    row = lambda v: v.reshape(1, -1).astype(_F32)

    own, adj = _bias_tables(rel_bias)
    xf = x.reshape(t, d_model).astype(_F32)
    for i in range(depth):
        w_u, w_q, w_k, w_v = jnp.split(
            w_in[i], [D_POOL, D_POOL + D_ATTN, D_POOL + 2 * D_ATTN], axis=-1)
        wqv_t = jnp.concatenate([w_q, w_v], axis=-1).T.astype(_BF16)
        k, qv3, ypool = _proj_pool(
            xf, row(g_mix_pre[i]), w_u.astype(_BF16), w_k.astype(_BF16), wqv_t,
            w_pool[i].astype(_BF16), row(pool_scale[i]), seq_len=seq_len)
        k3 = k.reshape(t // MOBA_BLOCK, MOBA_BLOCK, D_ATTN)
        yattn = _moba(qv3, k3, own, adj, batch=batch, seq_len=seq_len)
        xf = _ffn(
            xf, p[i].reshape(t, PLE_DIM).astype(_F32), ypool, yattn,
            w_out[i, :D_POOL].astype(_BF16), w_out[i, D_POOL:].astype(_BF16),
            row(g_mix_post[i]), row(g_ffn_pre[i]),
            _chunk_cols(w_up[i, :, :D_FF]).astype(_BF16),
            _chunk_cols(w_up[i, :, D_FF:]).astype(_BF16),
            conv_w[i].astype(_F32), row(conv_b[i]),
            w_down[i].reshape(N_FF_CHUNKS, FF_CHUNK, D_MODEL).astype(_BF16),
            row(g_ffn_post[i]), w_ple[i].astype(_BF16), w_ple_gate[i].astype(_BF16),
            seq_len=seq_len)
    return xf.reshape(batch, seq_len, d_model).astype(x.dtype)
```

```python
import functools
import math

import numpy as np
import jax
import jax.numpy as jnp
from jax import lax
from jax.experimental import pallas as pl
from jax.experimental.pallas import tpu as pltpu

D_MODEL = 1024
PLE_DIM = 256
D_POOL = 512
POOL_WINDOWS = (2, 4, 8, 16)
POOL_GROUP = D_POOL // len(POOL_WINDOWS)
D_ATTN = D_MODEL - D_POOL
N_HEADS = 8
HEAD_DIM = D_ATTN // N_HEADS
MOBA_BLOCK = 256
MOBA_TOPK = 3
NUM_BUCKETS = 32
MAX_DISTANCE = 128
D_FF = 2816
CONV_WIDTH = 3
EPS = 1e-6

HEADS_PER_STEP = 2
HEAD_PAIRS = N_HEADS // HEADS_PER_STEP
POOL_HALO = 16
CONV_HALO = 8
FF_CHUNK = 256
N_FF_CHUNKS = D_FF // FF_CHUNK
ROW_TILE = 512
MASKED = -1e30
LOG2_E = math.log2(math.e)
BLOCKS_PER_CHUNK = 2
KEY_CHUNK = BLOCKS_PER_CHUNK * MOBA_BLOCK
VMEM_LIMIT = 56 * 1024 * 1024

_BF16 = jnp.bfloat16
_F32 = jnp.float32


def _rms(x, g):
    return x * lax.rsqrt(jnp.mean(x * x, axis=-1, keepdims=True) + EPS) * g


def _dot(a, b):
    return jnp.dot(a, b, preferred_element_type=_F32)


def _bucket_tables():
    key = np.arange(MOBA_BLOCK)[:, None]
    qry = np.arange(MOBA_BLOCK)[None, :]

    def bucket(dist):
        n = np.maximum(dist, 0)
        max_exact = NUM_BUCKETS // 2
        nf = np.maximum(n, 1).astype(np.float32)
        large = max_exact + (np.log(nf / max_exact) / math.log(MAX_DISTANCE / max_exact)
                             * (NUM_BUCKETS - max_exact)).astype(np.int32)
        large = np.minimum(large, NUM_BUCKETS - 1)
        return np.where(n < max_exact, n, large).astype(np.int32)

    d_own = qry - key
    own = np.where(d_own >= 0, bucket(d_own), -1).astype(np.int32)
    adj = bucket(d_own + MOBA_BLOCK)
    return own, adj


def _bias_table_kernel(rb_ref, own_idx_ref, adj_idx_ref, tab_ref):
    h = pl.program_id(0)
    far = rb_ref[NUM_BUCKETS - 1, h]
    own_idx = own_idx_ref[...]
    adj_idx = adj_idx_ref[...]
    own = jnp.zeros(own_idx.shape, _F32)
    adj = jnp.zeros(adj_idx.shape, _F32)
    for b in range(NUM_BUCKETS):
        val = (rb_ref[b, h] - far) * LOG2_E
        own = jnp.where(own_idx == b, val, own)
        adj = jnp.where(adj_idx == b, val, adj)
    own = jnp.where(own_idx < 0, MASKED, own)
    tab_ref[0, 0, 0:MOBA_BLOCK, :] = adj
    tab_ref[0, 0, MOBA_BLOCK:, :] = own
    tab_ref[0, 1, 0:MOBA_BLOCK, :] = own
    tab_ref[0, 1, MOBA_BLOCK:, :] = jnp.full(own.shape, MASKED, _F32)


def _bias_tables(rel_bias):
    own_idx, adj_idx = _bucket_tables()
    blk = pl.BlockSpec((MOBA_BLOCK, MOBA_BLOCK), lambda h: (0, 0))
    return pl.pallas_call(
        _bias_table_kernel,
        grid=(N_HEADS,),
        in_specs=[pl.BlockSpec(memory_space=pltpu.SMEM), blk, blk],
        out_specs=pl.BlockSpec((1, 2, KEY_CHUNK, MOBA_BLOCK), lambda h: (h, 0, 0, 0)),
        out_shape=jax.ShapeDtypeStruct((N_HEADS, 2, KEY_CHUNK, MOBA_BLOCK), _F32),
        name="bias_tables",
    )(rel_bias.astype(_F32), jnp.asarray(own_idx), jnp.asarray(adj_idx))


def _proj_pool_kernel(x_ref, g_ref, wu_ref, wk_ref, wqv_ref, wpool_ref, pscale_ref,
                      k_ref, qv_ref, ypool_ref, ext_ref, *, tiles_per_seq):
    i = pl.program_id(0)
    tm = x_ref.shape[0]
    h = _rms(x_ref[...], g_ref[...]).astype(_BF16)

    k_ref[...] = _dot(h, wk_ref[...]).astype(_BF16)

    qv = lax.dot_general(wqv_ref[...], h, (((1,), (1,)), ((), ())),
                         preferred_element_type=_F32)
    row = lax.broadcasted_iota(jnp.int32, qv.shape, 0)
    qv = jnp.where(row < D_ATTN, qv * (HEAD_DIM ** -0.5 * LOG2_E), qv).astype(_BF16)
    for c in range(tm // MOBA_BLOCK):
        qv_ref[c] = qv[:, c * MOBA_BLOCK:(c + 1) * MOBA_BLOCK]

    u = _dot(h, wu_ref[...])
    seq_tile = i % tiles_per_seq

    @pl.when(seq_tile == 0)
    def _():
        ext_ref[0:POOL_HALO, :] = jnp.zeros((POOL_HALO, D_POOL), _F32)

    ext_ref[POOL_HALO:POOL_HALO + tm, :] = u
    pos = seq_tile * tm + lax.broadcasted_iota(jnp.int32, (tm, 1), 0)
    for g, w in enumerate(POOL_WINDOWS):
        cols = slice(g * POOL_GROUP, (g + 1) * POOL_GROUP)
        u_g = u[:, cols]
        acc = u_g
        for lag in range(1, w):
            acc = acc + ext_ref[POOL_HALO - lag:POOL_HALO - lag + tm, cols]
        count = jnp.minimum(pos + 1, w).astype(_F32)
        pooled = (acc / count - u_g).astype(_BF16)
        mixed = _dot(pooled, wpool_ref[g]) * pscale_ref[:, cols]
        ypool_ref[:, cols] = mixed.astype(_BF16)
    ext_ref[0:POOL_HALO, :] = u[tm - POOL_HALO:, :]


def _proj_pool(x2d, g, wu, wk, wqv, wpool, pscale, *, seq_len):
    t = x2d.shape[0]
    tm = ROW_TILE
    const = lambda shape: pl.BlockSpec(shape, lambda i: (0,) * len(shape))
    return pl.pallas_call(
        functools.partial(_proj_pool_kernel, tiles_per_seq=seq_len // tm),
        grid=(t // tm,),
        in_specs=[
            pl.BlockSpec((tm, D_MODEL), lambda i: (i, 0)),
            const((1, D_MODEL)),
            const((D_MODEL, D_POOL)),
            const((D_MODEL, D_ATTN)),
            const((2 * D_ATTN, D_MODEL)),
            const((len(POOL_WINDOWS), POOL_GROUP, POOL_GROUP)),
            const((1, D_POOL)),
        ],
        out_specs=[
            pl.BlockSpec((tm, D_ATTN), lambda i: (i, 0)),
            pl.BlockSpec((tm // MOBA_BLOCK, 2 * D_ATTN, MOBA_BLOCK), lambda i: (i, 0, 0)),
            pl.BlockSpec((tm, D_POOL), lambda i: (i, 0)),
        ],
        out_shape=[
            jax.ShapeDtypeStruct((t, D_ATTN), _BF16),
            jax.ShapeDtypeStruct((t // MOBA_BLOCK, 2 * D_ATTN, MOBA_BLOCK), _BF16),
            jax.ShapeDtypeStruct((t, D_POOL), _BF16),
        ],
        scratch_shapes=[pltpu.VMEM((POOL_HALO + tm, D_POOL), _F32)],
        compiler_params=pltpu.CompilerParams(
            dimension_semantics=("arbitrary",), vmem_limit_bytes=VMEM_LIMIT),
        name="proj_pool",
    )(x2d, g, wu, wk, wqv, wpool, pscale)


def _select_blocks(gate, n_past):
    blk = lax.broadcasted_iota(jnp.int32, gate.shape, 0)
    n_blocks = gate.shape[0]
    g = jnp.where(blk < n_past, gate, -jnp.inf)
    sel = jnp.zeros(gate.shape, _F32)
    for _ in range(MOBA_TOPK):
        mx = jnp.max(g, axis=0, keepdims=True)
        is_max = jnp.logical_and(g == mx, mx > -jnp.inf)
        first = jnp.min(jnp.where(is_max, blk, n_blocks), axis=0, keepdims=True)
        pick = blk == first
        sel = jnp.where(pick, 1.0, sel)
        g = jnp.where(pick, -jnp.inf, g)
    return sel


def _moba_kernel(q_ref, k_ref, v_ref, tab_ref, o_ref, kmean_ref, fsel_ref, s_ref):
    j = pl.program_id(2)
    nb = v_ref.shape[0]

    @pl.when(j == 0)
    def _():
        for n in range(nb):
            kb = k_ref[n * MOBA_BLOCK:(n + 1) * MOBA_BLOCK, :].astype(_F32)
            kmean_ref[n:n + 1, :] = jnp.sum(kb, axis=0, keepdims=True) * (1.0 / MOBA_BLOCK)

    q_t = q_ref[0]
    tq = q_t.shape[1]
    feat = lax.broadcasted_iota(jnp.int32, q_t.shape, 0)
    zero = jnp.zeros_like(q_t)
    q_heads = [jnp.where(feat < HEAD_DIM, q_t, zero), jnp.where(feat >= HEAD_DIM, q_t, zero)]

    kmean = kmean_ref[...].astype(_BF16)
    blk = lax.broadcasted_iota(jnp.int32, (nb, tq), 0)
    adj_flag = []
    for a in range(HEADS_PER_STEP):
        sel = _select_blocks(_dot(kmean, q_heads[a]), j)
        is_adj = blk == j - 1
        fsel_ref[a] = jnp.where(is_adj, 0.0, sel)
        flag = jnp.max(jnp.where(is_adj, sel, 0.0), axis=0, keepdims=True)
        adj_flag.append(jnp.where(j == 0, 1.0, flag))

    first = jnp.maximum(j - 1, 0)
    variant = jnp.where(j == 0, 1, 0)
    n_far_chunks = lax.shift_right_logical(j, 1)

    kb = k_ref[pl.ds(pl.multiple_of(first * MOBA_BLOCK, MOBA_BLOCK), KEY_CHUNK), :]
    m0 = []
    for a in range(HEADS_PER_STEP):
        s = _dot(kb, q_heads[a]) + tab_ref[a, variant]
        top = jnp.where(adj_flag[a] > 0.0, s[:MOBA_BLOCK], MASKED)
        s = jnp.concatenate([top, s[MOBA_BLOCK:]], axis=0)
        s_ref[a, 0:KEY_CHUNK, :] = s
        m0.append(jnp.max(s, axis=0, keepdims=True))

    def score_chunk(c, m):
        kb = k_ref[pl.ds(pl.multiple_of(c * KEY_CHUNK, KEY_CHUNK), KEY_CHUNK), :]
        out = []
        for a in range(HEADS_PER_STEP):
            s = _dot(kb, q_heads[a])
            parts = []
            for r in range(BLOCKS_PER_CHUNK):
                flag = fsel_ref[a, pl.ds(c * BLOCKS_PER_CHUNK + r, 1), :]
                parts.append(jnp.where(flag > 0.0, s[r * MOBA_BLOCK:(r + 1) * MOBA_BLOCK], MASKED))
            s = jnp.concatenate(parts, axis=0)
            s_ref[a, pl.ds(pl.multiple_of((c + 1) * KEY_CHUNK, KEY_CHUNK), KEY_CHUNK), :] = s
            out.append(jnp.maximum(m[a], jnp.max(s, axis=0, keepdims=True)))
        return tuple(out)

    m = lax.fori_loop(0, n_far_chunks, score_chunk, tuple(m0))

    def weigh_chunk(slot, first_block, state):
        new_state = []
        for a in range(HEADS_PER_STEP):
            l, acc = state[a]
            s = s_ref[a, pl.ds(pl.multiple_of(slot * KEY_CHUNK, KEY_CHUNK), KEY_CHUNK), :]
            p = jnp.exp2(s - m[a])
            l = l + jnp.sum(p, axis=0, keepdims=True)
            p = p.astype(_BF16)
            for r in range(BLOCKS_PER_CHUNK):
                v_t = v_ref[first_block + r, a * HEAD_DIM:(a + 1) * HEAD_DIM, :]
                acc = acc + _dot(v_t, p[r * MOBA_BLOCK:(r + 1) * MOBA_BLOCK])
            new_state.append((l, acc))
        return tuple(new_state)

    init = tuple((jnp.zeros((1, tq), _F32), jnp.zeros((HEAD_DIM, tq), _F32))
                 for _ in range(HEADS_PER_STEP))
    state = weigh_chunk(0, first, init)
    state = lax.fori_loop(
        0, n_far_chunks, lambda c, st: weigh_chunk(c + 1, c * BLOCKS_PER_CHUNK, st), state)

    out_t = jnp.concatenate([acc / l for (l, acc) in state], axis=0)
    o_ref[...] = out_t.T.astype(_BF16)


def _moba(qv3, k, tables, *, batch, seq_len):
    nb = seq_len // MOBA_BLOCK
    t = batch * seq_len
    pair = HEADS_PER_STEP * HEAD_DIM
    assert nb % BLOCKS_PER_CHUNK == 0
    return pl.pallas_call(
        _moba_kernel,
        grid=(batch, HEAD_PAIRS, nb),
        in_specs=[
            pl.BlockSpec((1, pair, MOBA_BLOCK), lambda b, hp, j: (b * nb + j, hp, 0)),
            pl.BlockSpec((seq_len, pair), lambda b, hp, j: (b, hp)),
            pl.BlockSpec((nb, pair, MOBA_BLOCK), lambda b, hp, j: (b, HEAD_PAIRS + hp, 0)),
            pl.BlockSpec((HEADS_PER_STEP, 2, KEY_CHUNK, MOBA_BLOCK),
                         lambda b, hp, j: (hp, 0, 0, 0)),
        ],
        out_specs=pl.BlockSpec((MOBA_BLOCK, pair), lambda b, hp, j: (b * nb + j, hp)),
        out_shape=jax.ShapeDtypeStruct((t, D_ATTN), _BF16),
        scratch_shapes=[pltpu.VMEM((nb, pair), _F32),
                        pltpu.VMEM((HEADS_PER_STEP, nb, MOBA_BLOCK), _F32),
                        pltpu.VMEM((HEADS_PER_STEP, seq_len, MOBA_BLOCK), _F32)],
        compiler_params=pltpu.CompilerParams(
            dimension_semantics=("arbitrary", "arbitrary", "arbitrary"),
            vmem_limit_bytes=VMEM_LIMIT),
        name="moba_attention",
    )(qv3, k, qv3, tables)


def _shift_rows(g, prev_rows, lag):
    rolled = pltpu.roll(g, lag, 0)
    rid = lax.broadcasted_iota(jnp.int32, prev_rows.shape, 0)
    head = jnp.where(rid < lag, pltpu.roll(prev_rows, lag, 0), rolled[:CONV_HALO])
    return jnp.concatenate([head, rolled[CONV_HALO:]], axis=0)


def _ffn_kernel(x_ref, p_ref, ypool_ref, yattn_ref, wo_pool_ref, wo_attn_ref, g_post_ref,
                g_pre_ref, wgate_ref, wval_ref, convw_ref, convb_ref, wdown_ref, g_ffn_post_ref,
                wple_ref, wplegate_ref, o_ref, carry_ref, acc_ref, *, tiles_per_seq):
    i = pl.program_id(0)
    tm = x_ref.shape[0]

    @pl.when(i % tiles_per_seq == 0)
    def _():
        carry_ref[...] = jnp.zeros(carry_ref.shape, _F32)

    y = _dot(ypool_ref[...], wo_pool_ref[...]) + _dot(yattn_ref[...], wo_attn_ref[...])
    x1 = x_ref[...] + _rms(y, g_post_ref[...])
    h = _rms(x1, g_pre_ref[...]).astype(_BF16)

    for c in range(N_FF_CHUNKS):
        cols = slice(c * FF_CHUNK, (c + 1) * FF_CHUNK)
        gate = _dot(h, wgate_ref[c])
        val = _dot(h, wval_ref[c])
        prev = carry_ref[:, cols]
        conv = convb_ref[:, cols] + _shift_rows(gate, prev, 2) * convw_ref[0:1, cols]
        conv = conv + _shift_rows(gate, prev, 1) * convw_ref[1:2, cols]
        conv = conv + gate * convw_ref[2:3, cols]
        carry_ref[:, cols] = gate[tm - CONV_HALO:, :]
        act = (jax.nn.gelu(conv, approximate=True) * val).astype(_BF16)
        down = _dot(act, wdown_ref[c])
        if c == 0:
            acc_ref[...] = down
        else:
            acc_ref[...] += down

    x2 = x1 + _rms(acc_ref[...], g_ffn_post_ref[...])
    ple = _dot(p_ref[...].astype(_BF16), wple_ref[...])
    gate_logit = _dot(x2.astype(_BF16), wplegate_ref[...])
    o_ref[...] = x2 + ple * (1.0 / (1.0 + jnp.exp(-gate_logit)))


def _ffn(x2d, p2d, ypool, yattn, wo_pool, wo_attn, g_post, g_pre, wgate, wval, convw, convb,
         wdown, g_ffn_post, wple, wplegate, *, seq_len):
    t = x2d.shape[0]
    tm = ROW_TILE
    rows = lambda width: pl.BlockSpec((tm, width), lambda i: (i, 0))
    const = lambda shape: pl.BlockSpec(shape, lambda i: (0,) * len(shape),
                                       pipeline_mode=pl.Buffered(1))
    return pl.pallas_call(
        functools.partial(_ffn_kernel, tiles_per_seq=seq_len // tm),
        grid=(t // tm,),
        in_specs=[
            rows(D_MODEL), rows(PLE_DIM), rows(D_POOL), rows(D_ATTN),
            const((D_POOL, D_MODEL)), const((D_ATTN, D_MODEL)), const((1, D_MODEL)),
            const((1, D_MODEL)),
            const((N_FF_CHUNKS, D_MODEL, FF_CHUNK)), const((N_FF_CHUNKS, D_MODEL, FF_CHUNK)),
            const((CONV_WIDTH, D_FF)), const((1, D_FF)),
            const((N_FF_CHUNKS, FF_CHUNK, D_MODEL)), const((1, D_MODEL)),
            const((PLE_DIM, D_MODEL)), const((D_MODEL, D_MODEL)),
        ],
        out_specs=rows(D_MODEL),
        out_shape=jax.ShapeDtypeStruct((t, D_MODEL), _F32),
        scratch_shapes=[pltpu.VMEM((CONV_HALO, D_FF), _F32), pltpu.VMEM((tm, D_MODEL), _F32)],
        compiler_params=pltpu.CompilerParams(
            dimension_semantics=("arbitrary",), vmem_limit_bytes=VMEM_LIMIT),
        name="ffn_block",
    )(x2d, p2d, ypool, yattn, wo_pool, wo_attn, g_post, g_pre, wgate, wval, convw, convb,
      wdown, g_ffn_post, wple, wplegate)


def _chunk_cols(w):
    return w.reshape(w.shape[0], N_FF_CHUNKS, FF_CHUNK).transpose(1, 0, 2)


def kernel(x, p, rel_bias, g_mix_pre, g_mix_post, g_ffn_pre, g_ffn_post, w_in, w_pool,
           pool_scale, w_out, w_up, conv_w, conv_b, w_down, w_ple, w_ple_gate):
    batch, seq_len, d_model = x.shape
    depth = w_in.shape[0]
    assert d_model == D_MODEL and seq_len % ROW_TILE == 0 and ROW_TILE % MOBA_BLOCK == 0
    t = batch * seq_len
    row = lambda v: v.reshape(1, -1).astype(_F32)

    tables = _bias_tables(rel_bias)
    xf = x.reshape(t, d_model).astype(_F32)
    for i in range(depth):
        w_u, w_q, w_k, w_v = jnp.split(
            w_in[i], [D_POOL, D_POOL + D_ATTN, D_POOL + 2 * D_ATTN], axis=-1)
        wqv_t = jnp.concatenate([w_q, w_v], axis=-1).T.astype(_BF16)
        k, qv3, ypool = _proj_pool(
            xf, row(g_mix_pre[i]), w_u.astype(_BF16), w_k.astype(_BF16), wqv_t,
            w_pool[i].astype(_BF16), row(pool_scale[i]), seq_len=seq_len)
        yattn = _moba(qv3, k, tables, batch=batch, seq_len=seq_len)
        xf = _ffn(
            xf, p[i].reshape(t, PLE_DIM).astype(_F32), ypool, yattn,
            w_out[i, :D_POOL].astype(_BF16), w_out[i, D_POOL:].astype(_BF16),
            row(g_mix_post[i]), row(g_ffn_pre[i]),
            _chunk_cols(w_up[i, :, :D_FF]).astype(_BF16),
            _chunk_cols(w_up[i, :, D_FF:]).astype(_BF16),
            conv_w[i].astype(_F32), row(conv_b[i]),
            w_down[i].reshape(N_FF_CHUNKS, FF_CHUNK, D_MODEL).astype(_BF16),
            row(g_ffn_post[i]), w_ple[i].astype(_BF16), w_ple_gate[i].astype(_BF16),
            seq_len=seq_len)
    return xf.reshape(batch, seq_len, d_model).astype(x.dtype)
```

```python
import functools
import math

import numpy as np
import jax
import jax.numpy as jnp
from jax import lax
from jax.experimental import pallas as pl
from jax.experimental.pallas import tpu as pltpu

D_MODEL = 1024
PLE_DIM = 256
D_POOL = 512
POOL_WINDOWS = (2, 4, 8, 16)
POOL_GROUP = D_POOL // len(POOL_WINDOWS)
D_ATTN = D_MODEL - D_POOL
N_HEADS = 8
HEAD_DIM = D_ATTN // N_HEADS
MOBA_BLOCK = 256
MOBA_TOPK = 3
NUM_BUCKETS = 32
MAX_DISTANCE = 128
D_FF = 2816
CONV_WIDTH = 3
EPS = 1e-6

HEADS_PER_STEP = 4
HEAD_GROUPS = N_HEADS // HEADS_PER_STEP
POOL_HALO = 16
CONV_HALO = 8
FF_CHUNK = 256
N_FF_CHUNKS = D_FF // FF_CHUNK
ROW_TILE = 512
MASKED = -1e30
LOG2_E = math.log2(math.e)
BLOCKS_PER_CHUNK = 2
KEY_CHUNK = BLOCKS_PER_CHUNK * MOBA_BLOCK
SUM_ROWS = 16
VMEM_LIMIT = 56 * 1024 * 1024

_BF16 = jnp.bfloat16
_F32 = jnp.float32


def _rms(x, g):
    return x * lax.rsqrt(jnp.mean(x * x, axis=-1, keepdims=True) + EPS) * g


def _dot(a, b):
    return jnp.dot(a, b, preferred_element_type=_F32)


def _bucket_tables():
    key = np.arange(MOBA_BLOCK)[:, None]
    qry = np.arange(MOBA_BLOCK)[None, :]

    def bucket(dist):
        n = np.maximum(dist, 0)
        max_exact = NUM_BUCKETS // 2
        nf = np.maximum(n, 1).astype(np.float32)
        large = max_exact + (np.log(nf / max_exact) / math.log(MAX_DISTANCE / max_exact)
                             * (NUM_BUCKETS - max_exact)).astype(np.int32)
        large = np.minimum(large, NUM_BUCKETS - 1)
        return np.where(n < max_exact, n, large).astype(np.int32)

    d_own = qry - key
    own = np.where(d_own >= 0, bucket(d_own), -1).astype(np.int32)
    adj = bucket(d_own + MOBA_BLOCK)
    return own, adj


def _bias_table_kernel(rb_ref, own_idx_ref, adj_idx_ref, tab_ref):
    h = pl.program_id(0)
    far = rb_ref[NUM_BUCKETS - 1, h]
    own_idx = own_idx_ref[...]
    adj_idx = adj_idx_ref[...]
    own = jnp.zeros(own_idx.shape, _F32)
    adj = jnp.zeros(adj_idx.shape, _F32)
    for b in range(NUM_BUCKETS):
        val = (rb_ref[b, h] - far) * LOG2_E
        own = jnp.where(own_idx == b, val, own)
        adj = jnp.where(adj_idx == b, val, adj)
    own = jnp.where(own_idx < 0, MASKED, own)
    tab_ref[0, 0, 0:MOBA_BLOCK, :] = adj
    tab_ref[0, 0, MOBA_BLOCK:, :] = own
    tab_ref[0, 1, 0:MOBA_BLOCK, :] = own
    tab_ref[0, 1, MOBA_BLOCK:, :] = jnp.full(own.shape, MASKED, _F32)


def _bias_tables(rel_bias):
    own_idx, adj_idx = _bucket_tables()
    blk = pl.BlockSpec((MOBA_BLOCK, MOBA_BLOCK), lambda h: (0, 0))
    return pl.pallas_call(
        _bias_table_kernel,
        grid=(N_HEADS,),
        in_specs=[pl.BlockSpec(memory_space=pltpu.SMEM), blk, blk],
        out_specs=pl.BlockSpec((1, 2, KEY_CHUNK, MOBA_BLOCK), lambda h: (h, 0, 0, 0)),
        out_shape=jax.ShapeDtypeStruct((N_HEADS, 2, KEY_CHUNK, MOBA_BLOCK), _F32),
        name="bias_tables",
    )(rel_bias.astype(_F32), jnp.asarray(own_idx), jnp.asarray(adj_idx))


def _proj_pool_kernel(x_ref, g_ref, wu_ref, wk_ref, wqv_ref, wpool_ref, pscale_ref,
                      k_ref, qv_ref, ypool_ref, ext_ref, *, tiles_per_seq):
    i = pl.program_id(0)
    tm = x_ref.shape[0]
    h = _rms(x_ref[...], g_ref[...]).astype(_BF16)

    k_ref[...] = _dot(h, wk_ref[...]).astype(_BF16)

    qv = lax.dot_general(wqv_ref[...], h, (((1,), (1,)), ((), ())),
                         preferred_element_type=_F32)
    row = lax.broadcasted_iota(jnp.int32, qv.shape, 0)
    qv = jnp.where(row < D_ATTN, qv * (HEAD_DIM ** -0.5 * LOG2_E), qv).astype(_BF16)
    for c in range(tm // MOBA_BLOCK):
        qv_ref[c] = qv[:, c * MOBA_BLOCK:(c + 1) * MOBA_BLOCK]

    u = _dot(h, wu_ref[...])
    seq_tile = i % tiles_per_seq

    @pl.when(seq_tile == 0)
    def _():
        ext_ref[0:POOL_HALO, :] = jnp.zeros((POOL_HALO, D_POOL), _F32)

    ext_ref[POOL_HALO:POOL_HALO + tm, :] = u
    pos = seq_tile * tm + lax.broadcasted_iota(jnp.int32, (tm, 1), 0)
    for g, w in enumerate(POOL_WINDOWS):
        cols = slice(g * POOL_GROUP, (g + 1) * POOL_GROUP)
        u_g = u[:, cols]
        acc = u_g
        for lag in range(1, w):
            acc = acc + ext_ref[POOL_HALO - lag:POOL_HALO - lag + tm, cols]
        count = jnp.minimum(pos + 1, w).astype(_F32)
        pooled = (acc / count - u_g).astype(_BF16)
        mixed = _dot(pooled, wpool_ref[g]) * pscale_ref[:, cols]
        ypool_ref[:, cols] = mixed.astype(_BF16)
    ext_ref[0:POOL_HALO, :] = u[tm - POOL_HALO:, :]


def _proj_pool(x2d, g, wu, wk, wqv, wpool, pscale, *, seq_len):
    t = x2d.shape[0]
    tm = ROW_TILE
    const = lambda shape: pl.BlockSpec(shape, lambda i: (0,) * len(shape))
    return pl.pallas_call(
        functools.partial(_proj_pool_kernel, tiles_per_seq=seq_len // tm),
        grid=(t // tm,),
        in_specs=[
            pl.BlockSpec((tm, D_MODEL), lambda i: (i, 0)),
            const((1, D_MODEL)),
            const((D_MODEL, D_POOL)),
            const((D_MODEL, D_ATTN)),
            const((2 * D_ATTN, D_MODEL)),
            const((len(POOL_WINDOWS), POOL_GROUP, POOL_GROUP)),
            const((1, D_POOL)),
        ],
        out_specs=[
            pl.BlockSpec((tm, D_ATTN), lambda i: (i, 0)),
            pl.BlockSpec((tm // MOBA_BLOCK, 2 * D_ATTN, MOBA_BLOCK), lambda i: (i, 0, 0)),
            pl.BlockSpec((tm, D_POOL), lambda i: (i, 0)),
        ],
        out_shape=[
            jax.ShapeDtypeStruct((t, D_ATTN), _BF16),
            jax.ShapeDtypeStruct((t // MOBA_BLOCK, 2 * D_ATTN, MOBA_BLOCK), _BF16),
            jax.ShapeDtypeStruct((t, D_POOL), _BF16),
        ],
        scratch_shapes=[pltpu.VMEM((POOL_HALO + tm, D_POOL), _F32)],
        compiler_params=pltpu.CompilerParams(
            dimension_semantics=("arbitrary",), vmem_limit_bytes=VMEM_LIMIT),
        name="proj_pool",
    )(x2d, g, wu, wk, wqv, wpool, pscale)


def _select_blocks(gate, n_past):
    blk = lax.broadcasted_iota(jnp.int32, gate.shape, 0)
    n_blocks = gate.shape[0]
    g = jnp.where(blk < n_past, gate, -jnp.inf)
    sel = jnp.zeros(gate.shape, _F32)
    for _ in range(MOBA_TOPK):
        mx = jnp.max(g, axis=0, keepdims=True)
        is_max = jnp.logical_and(g == mx, mx > -jnp.inf)
        first = jnp.min(jnp.where(is_max, blk, n_blocks), axis=0, keepdims=True)
        pick = blk == first
        sel = jnp.where(pick, 1.0, sel)
        g = jnp.where(pick, -jnp.inf, g)
    return sel


def _moba_kernel(q_ref, k_ref, v_ref, tab_ref, o_ref, kmean_ref, fsel_ref, qh_ref, s_ref,
                 acc_ref):
    j = pl.program_id(2)
    nb = v_ref.shape[0]

    @pl.when(j == 0)
    def _():
        for n in range(nb):
            kb = k_ref[n * MOBA_BLOCK:(n + 1) * MOBA_BLOCK, :].astype(_F32)
            kmean_ref[n:n + 1, :] = jnp.sum(kb, axis=0, keepdims=True) * (1.0 / MOBA_BLOCK)

    q_t = q_ref[0]
    tq = q_t.shape[1]
    feat = lax.broadcasted_iota(jnp.int32, q_t.shape, 0)
    zero = jnp.zeros_like(q_t)
    for a in range(HEADS_PER_STEP):
        in_head = jnp.logical_and(feat >= a * HEAD_DIM, feat < (a + 1) * HEAD_DIM)
        qh_ref[a] = jnp.where(in_head, q_t, zero)

    kmean = kmean_ref[...].astype(_BF16)
    blk = lax.broadcasted_iota(jnp.int32, (nb, tq), 0)
    adj_flag = []
    for a in range(HEADS_PER_STEP):
        sel = _select_blocks(_dot(kmean, qh_ref[a]), j)
        is_adj = blk == j - 1
        fsel_ref[a] = jnp.where(is_adj, 0.0, sel)
        flag = jnp.max(jnp.where(is_adj, sel, 0.0), axis=0, keepdims=True)
        adj_flag.append(jnp.where(j == 0, 1.0, flag))

    first = jnp.maximum(j - 1, 0)
    variant = jnp.where(j == 0, 1, 0)
    n_far_chunks = lax.shift_right_logical(j, 1)

    kb = k_ref[pl.ds(pl.multiple_of(first * MOBA_BLOCK, MOBA_BLOCK), KEY_CHUNK), :]
    cmax0 = []
    for a in range(HEADS_PER_STEP):
        s = _dot(kb, qh_ref[a]) + tab_ref[a, variant]
        top = jnp.where(adj_flag[a] > 0.0, s[:MOBA_BLOCK], MASKED)
        s = jnp.concatenate([top, s[MOBA_BLOCK:]], axis=0)
        s_ref[0, a] = s
        cmax0.append(jnp.max(s, axis=0, keepdims=True))
        acc_ref[a] = jnp.zeros((HEAD_DIM, tq), _F32)

    ones = jnp.ones((SUM_ROWS, MOBA_BLOCK), _BF16)

    def score_far_chunk(i, slot):
        kb = k_ref[pl.ds(pl.multiple_of(i * KEY_CHUNK, KEY_CHUNK), KEY_CHUNK), :]
        cmax = []
        for a in range(HEADS_PER_STEP):
            s = _dot(kb, qh_ref[a])
            parts = []
            for r in range(BLOCKS_PER_CHUNK):
                flag = fsel_ref[a, pl.ds(i * BLOCKS_PER_CHUNK + r, 1), :]
                parts.append(jnp.where(flag > 0.0, s[r * MOBA_BLOCK:(r + 1) * MOBA_BLOCK], MASKED))
            s = jnp.concatenate(parts, axis=0)
            s_ref[slot, a] = s
            cmax.append(jnp.max(s, axis=0, keepdims=True))
        return tuple(cmax)

    def fold_chunk(slot, first_block, cmax, m, l):
        new_m, new_l = [], []
        for a in range(HEADS_PER_STEP):
            m_new = jnp.maximum(m[a], cmax[a])
            alpha = jnp.exp2(m[a] - m_new)
            p = jnp.exp2(s_ref[slot, a] - m_new).astype(_BF16)
            pv = None
            for r in range(BLOCKS_PER_CHUNK):
                v_t = jnp.concatenate(
                    [v_ref[first_block + r, a * HEAD_DIM:(a + 1) * HEAD_DIM, :], ones], axis=0)
                d = _dot(v_t, p[r * MOBA_BLOCK:(r + 1) * MOBA_BLOCK])
                pv = d if pv is None else pv + d
            acc_ref[a] = alpha * acc_ref[a] + pv[:HEAD_DIM]
            new_l.append(alpha * l[a] + pv[HEAD_DIM:HEAD_DIM + 1])
            new_m.append(m_new)
        return tuple(new_m), tuple(new_l)

    def step(t, carry):
        cmax, m, l = carry
        i = 2 * t
        cmax1 = score_far_chunk(i, 1)
        first_block = jnp.where(t == 0, first, (i - 1) * BLOCKS_PER_CHUNK)
        m, l = fold_chunk(0, first_block, cmax, m, l)
        cmax0 = score_far_chunk(i + 1, 0)
        m, l = fold_chunk(1, i * BLOCKS_PER_CHUNK, cmax1, m, l)
        return cmax0, m, l

    init_m = tuple(jnp.full((1, tq), MASKED, _F32) for _ in range(HEADS_PER_STEP))
    init_l = tuple(jnp.zeros((1, tq), _F32) for _ in range(HEADS_PER_STEP))
    n_trips = lax.shift_right_logical(n_far_chunks + 2, 1)
    _, _, l = lax.fori_loop(0, n_trips, step, (tuple(cmax0), init_m, init_l))

    out_t = jnp.concatenate([acc_ref[a] / l[a] for a in range(HEADS_PER_STEP)], axis=0)
    o_ref[...] = out_t.T.astype(_BF16)


def _moba(qv3, k, tables, *, batch, seq_len):
    nb = seq_len // MOBA_BLOCK
    t = batch * seq_len
    width = HEADS_PER_STEP * HEAD_DIM
    assert nb % BLOCKS_PER_CHUNK == 0
    return pl.pallas_call(
        _moba_kernel,
        grid=(batch, HEAD_GROUPS, nb),
        in_specs=[
            pl.BlockSpec((1, width, MOBA_BLOCK), lambda b, hg, j: (b * nb + j, hg, 0)),
            pl.BlockSpec((seq_len, width), lambda b, hg, j: (b, hg)),
            pl.BlockSpec((nb, width, MOBA_BLOCK), lambda b, hg, j: (b, HEAD_GROUPS + hg, 0)),
            pl.BlockSpec((HEADS_PER_STEP, 2, KEY_CHUNK, MOBA_BLOCK),
                         lambda b, hg, j: (hg, 0, 0, 0)),
        ],
        out_specs=pl.BlockSpec((MOBA_BLOCK, width), lambda b, hg, j: (b * nb + j, hg)),
        out_shape=jax.ShapeDtypeStruct((t, D_ATTN), _BF16),
        scratch_shapes=[pltpu.VMEM((nb, width), _F32),
                        pltpu.VMEM((HEADS_PER_STEP, nb, MOBA_BLOCK), _F32),
                        pltpu.VMEM((HEADS_PER_STEP, width, MOBA_BLOCK), _BF16),
                        pltpu.VMEM((2, HEADS_PER_STEP, KEY_CHUNK, MOBA_BLOCK), _F32),
                        pltpu.VMEM((HEADS_PER_STEP, HEAD_DIM, MOBA_BLOCK), _F32)],
        compiler_params=pltpu.CompilerParams(
            dimension_semantics=("arbitrary", "arbitrary", "arbitrary"),
            vmem_limit_bytes=VMEM_LIMIT),
        name="moba_attention",
    )(qv3, k, qv3, tables)


def _shift_rows(g, prev_rows, lag):
    rolled = pltpu.roll(g, lag, 0)
    rid = lax.broadcasted_iota(jnp.int32, prev_rows.shape, 0)
    head = jnp.where(rid < lag, pltpu.roll(prev_rows, lag, 0), rolled[:CONV_HALO])
    return jnp.concatenate([head, rolled[CONV_HALO:]], axis=0)


def _ffn_kernel(x_ref, p_ref, ypool_ref, yattn_ref, wo_pool_ref, wo_attn_ref, g_post_ref,
                g_pre_ref, wgate_ref, wval_ref, convw_ref, convb_ref, wdown_ref, g_ffn_post_ref,
                wple_ref, wplegate_ref, o_ref, carry_ref, act_ref, *, tiles_per_seq):
    i = pl.program_id(0)
    tm = x_ref.shape[0]

    @pl.when(i % tiles_per_seq == 0)
    def _():
        carry_ref[...] = jnp.zeros(carry_ref.shape, _F32)

    y = _dot(ypool_ref[...], wo_pool_ref[...]) + _dot(yattn_ref[...], wo_attn_ref[...])
    x1 = x_ref[...] + _rms(y, g_post_ref[...])
    h = _rms(x1, g_pre_ref[...]).astype(_BF16)

    for c in range(N_FF_CHUNKS):
        cols = slice(c * FF_CHUNK, (c + 1) * FF_CHUNK)
        gate = _dot(h, wgate_ref[c])
        val = _dot(h, wval_ref[c])
        prev = carry_ref[:, cols]
        conv = convb_ref[:, cols] + _shift_rows(gate, prev, 2) * convw_ref[0:1, cols]
        conv = conv + _shift_rows(gate, prev, 1) * convw_ref[1:2, cols]
        conv = conv + gate * convw_ref[2:3, cols]
        carry_ref[:, cols] = gate[tm - CONV_HALO:, :]
        act_ref[:, cols] = (jax.nn.gelu(conv, approximate=True) * val).astype(_BF16)

    x2 = x1 + _rms(_dot(act_ref[...], wdown_ref[...]), g_ffn_post_ref[...])
    ple = _dot(p_ref[...].astype(_BF16), wple_ref[...])
    gate_logit = _dot(x2.astype(_BF16), wplegate_ref[...])
    o_ref[...] = x2 + ple * (1.0 / (1.0 + jnp.exp(-gate_logit)))


def _ffn(x2d, p2d, ypool, yattn, wo_pool, wo_attn, g_post, g_pre, wgate, wval, convw, convb,
         wdown, g_ffn_post, wple, wplegate, *, seq_len):
    t = x2d.shape[0]
    tm = ROW_TILE
    rows = lambda width: pl.BlockSpec((tm, width), lambda i: (i, 0))
    const = lambda shape: pl.BlockSpec(shape, lambda i: (0,) * len(shape),
                                       pipeline_mode=pl.Buffered(1))
    return pl.pallas_call(
        functools.partial(_ffn_kernel, tiles_per_seq=seq_len // tm),
        grid=(t // tm,),
        in_specs=[
            rows(D_MODEL), rows(PLE_DIM), rows(D_POOL), rows(D_ATTN),
            const((D_POOL, D_MODEL)), const((D_ATTN, D_MODEL)), const((1, D_MODEL)),
            const((1, D_MODEL)),
            const((N_FF_CHUNKS, D_MODEL, FF_CHUNK)), const((N_FF_CHUNKS, D_MODEL, FF_CHUNK)),
            const((CONV_WIDTH, D_FF)), const((1, D_FF)),
            const((D_FF, D_MODEL)), const((1, D_MODEL)),
            const((PLE_DIM, D_MODEL)), const((D_MODEL, D_MODEL)),
        ],
        out_specs=rows(D_MODEL),
        out_shape=jax.ShapeDtypeStruct((t, D_MODEL), _F32),
        scratch_shapes=[pltpu.VMEM((CONV_HALO, D_FF), _F32), pltpu.VMEM((tm, D_FF), _BF16)],
        compiler_params=pltpu.CompilerParams(
            dimension_semantics=("arbitrary",), vmem_limit_bytes=VMEM_LIMIT),
        name="ffn_block",
    )(x2d, p2d, ypool, yattn, wo_pool, wo_attn, g_post, g_pre, wgate, wval, convw, convb,
      wdown, g_ffn_post, wple, wplegate)


def _chunk_cols(w):
    return w.reshape(w.shape[0], N_FF_CHUNKS, FF_CHUNK).transpose(1, 0, 2)


def kernel(x, p, rel_bias, g_mix_pre, g_mix_post, g_ffn_pre, g_ffn_post, w_in, w_pool,
           pool_scale, w_out, w_up, conv_w, conv_b, w_down, w_ple, w_ple_gate):
    batch, seq_len, d_model = x.shape
    depth = w_in.shape[0]
    assert d_model == D_MODEL and seq_len % ROW_TILE == 0 and ROW_TILE % MOBA_BLOCK == 0
    t = batch * seq_len
    row = lambda v: v.reshape(1, -1).astype(_F32)

    tables = _bias_tables(rel_bias)
    xf = x.reshape(t, d_model).astype(_F32)
    for i in range(depth):
        w_u, w_q, w_k, w_v = jnp.split(
            w_in[i], [D_POOL, D_POOL + D_ATTN, D_POOL + 2 * D_ATTN], axis=-1)
        wqv_t = jnp.concatenate([w_q, w_v], axis=-1).T.astype(_BF16)
        k, qv3, ypool = _proj_pool(
            xf, row(g_mix_pre[i]), w_u.astype(_BF16), w_k.astype(_BF16), wqv_t,
            w_pool[i].astype(_BF16), row(pool_scale[i]), seq_len=seq_len)
        yattn = _moba(qv3, k, tables, batch=batch, seq_len=seq_len)
        xf = _ffn(
            xf, p[i].reshape(t, PLE_DIM).astype(_F32), ypool, yattn,
            w_out[i, :D_POOL].astype(_BF16), w_out[i, D_POOL:].astype(_BF16),
            row(g_mix_post[i]), row(g_ffn_pre[i]),
            _chunk_cols(w_up[i, :, :D_FF]).astype(_BF16),
            _chunk_cols(w_up[i, :, D_FF:]).astype(_BF16),
            conv_w[i].astype(_F32), row(conv_b[i]),
            w_down[i].astype(_BF16),
            row(g_ffn_post[i]), w_ple[i].astype(_BF16), w_ple_gate[i].astype(_BF16),
            seq_len=seq_len)
    return xf.reshape(batch, seq_len, d_model).astype(x.dtype)
```

```python
import functools
import math

import numpy as np
import jax
import jax.numpy as jnp
from jax import lax
from jax.experimental import pallas as pl
from jax.experimental.pallas import tpu as pltpu

D_MODEL = 1024
PLE_DIM = 256
D_POOL = 512
POOL_WINDOWS = (2, 4, 8, 16)
POOL_GROUP = D_POOL // len(POOL_WINDOWS)
D_ATTN = D_MODEL - D_POOL
N_HEADS = 8
HEAD_DIM = D_ATTN // N_HEADS
MOBA_BLOCK = 256
MOBA_TOPK = 3
NUM_BUCKETS = 32
MAX_DISTANCE = 128
D_FF = 2816
CONV_WIDTH = 3
EPS = 1e-6

HEADS_PER_STEP = N_HEADS
POOL_HALO = 16
CONV_HALO = 8
FF_CHUNK = 256
N_FF_CHUNKS = D_FF // FF_CHUNK
ROW_TILE = 512
MASKED = -1e30
LOG2_E = math.log2(math.e)
BLOCKS_PER_CHUNK = 2
KEY_CHUNK = BLOCKS_PER_CHUNK * MOBA_BLOCK
SUM_ROWS = 16
VMEM_LIMIT = 56 * 1024 * 1024

_BF16 = jnp.bfloat16
_F32 = jnp.float32


def _rms(x, g):
    return x * lax.rsqrt(jnp.mean(x * x, axis=-1, keepdims=True) + EPS) * g


def _dot(a, b):
    return jnp.dot(a, b, preferred_element_type=_F32)


def _bucket_tables():
    key = np.arange(MOBA_BLOCK)[:, None]
    qry = np.arange(MOBA_BLOCK)[None, :]

    def bucket(dist):
        n = np.maximum(dist, 0)
        max_exact = NUM_BUCKETS // 2
        nf = np.maximum(n, 1).astype(np.float32)
        large = max_exact + (np.log(nf / max_exact) / math.log(MAX_DISTANCE / max_exact)
                             * (NUM_BUCKETS - max_exact)).astype(np.int32)
        large = np.minimum(large, NUM_BUCKETS - 1)
        return np.where(n < max_exact, n, large).astype(np.int32)

    d_own = qry - key
    own = np.where(d_own >= 0, bucket(d_own), -1).astype(np.int32)
    adj = bucket(d_own + MOBA_BLOCK)
    return own, adj


def _bias_table_kernel(rb_ref, own_idx_ref, adj_idx_ref, tab_ref):
    h = pl.program_id(0)
    far = rb_ref[NUM_BUCKETS - 1, h]
    own_idx = own_idx_ref[...]
    adj_idx = adj_idx_ref[...]
    own = jnp.zeros(own_idx.shape, _F32)
    adj = jnp.zeros(adj_idx.shape, _F32)
    for b in range(NUM_BUCKETS):
        val = (rb_ref[b, h] - far) * LOG2_E
        own = jnp.where(own_idx == b, val, own)
        adj = jnp.where(adj_idx == b, val, adj)
    own = jnp.where(own_idx < 0, MASKED, own)
    tab_ref[0, 0, 0:MOBA_BLOCK, :] = adj
    tab_ref[0, 0, MOBA_BLOCK:, :] = own
    tab_ref[0, 1, 0:MOBA_BLOCK, :] = own
    tab_ref[0, 1, MOBA_BLOCK:, :] = jnp.full(own.shape, MASKED, _F32)


def _bias_tables(rel_bias):
    own_idx, adj_idx = _bucket_tables()
    blk = pl.BlockSpec((MOBA_BLOCK, MOBA_BLOCK), lambda h: (0, 0))
    return pl.pallas_call(
        _bias_table_kernel,
        grid=(N_HEADS,),
        in_specs=[pl.BlockSpec(memory_space=pltpu.SMEM), blk, blk],
        out_specs=pl.BlockSpec((1, 2, KEY_CHUNK, MOBA_BLOCK), lambda h: (h, 0, 0, 0)),
        out_shape=jax.ShapeDtypeStruct((N_HEADS, 2, KEY_CHUNK, MOBA_BLOCK), _F32),
        name="bias_tables",
    )(rel_bias.astype(_F32), jnp.asarray(own_idx), jnp.asarray(adj_idx))


def _proj_pool_kernel(x_ref, g_ref, wu_ref, wk_ref, wqv_ref, wpool_ref, pscale_ref,
                      k_ref, qv_ref, ypool_ref, ext_ref, *, tiles_per_seq):
    i = pl.program_id(0)
    tm = x_ref.shape[0]
    h = _rms(x_ref[...], g_ref[...]).astype(_BF16)

    k_ref[...] = _dot(h, wk_ref[...]).astype(_BF16)

    qv = lax.dot_general(wqv_ref[...], h, (((1,), (1,)), ((), ())),
                         preferred_element_type=_F32)
    row = lax.broadcasted_iota(jnp.int32, qv.shape, 0)
    qv = jnp.where(row < D_ATTN, qv * (HEAD_DIM ** -0.5 * LOG2_E), qv).astype(_BF16)
    for c in range(tm // MOBA_BLOCK):
        qv_ref[c] = qv[:, c * MOBA_BLOCK:(c + 1) * MOBA_BLOCK]

    u = _dot(h, wu_ref[...])
    seq_tile = i % tiles_per_seq

    @pl.when(seq_tile == 0)
    def _():
        ext_ref[0:POOL_HALO, :] = jnp.zeros((POOL_HALO, D_POOL), _F32)

    ext_ref[POOL_HALO:POOL_HALO + tm, :] = u
    pos = seq_tile * tm + lax.broadcasted_iota(jnp.int32, (tm, 1), 0)
    for g, w in enumerate(POOL_WINDOWS):
        cols = slice(g * POOL_GROUP, (g + 1) * POOL_GROUP)
        u_g = u[:, cols]
        acc = u_g
        for lag in range(1, w):
            acc = acc + ext_ref[POOL_HALO - lag:POOL_HALO - lag + tm, cols]
        count = jnp.minimum(pos + 1, w).astype(_F32)
        pooled = (acc / count - u_g).astype(_BF16)
        mixed = _dot(pooled, wpool_ref[g]) * pscale_ref[:, cols]
        ypool_ref[:, cols] = mixed.astype(_BF16)
    ext_ref[0:POOL_HALO, :] = u[tm - POOL_HALO:, :]


def _proj_pool(x2d, g, wu, wk, wqv, wpool, pscale, *, seq_len):
    t = x2d.shape[0]
    tm = ROW_TILE
    const = lambda shape: pl.BlockSpec(shape, lambda i: (0,) * len(shape))
    return pl.pallas_call(
        functools.partial(_proj_pool_kernel, tiles_per_seq=seq_len // tm),
        grid=(t // tm,),
        in_specs=[
            pl.BlockSpec((tm, D_MODEL), lambda i: (i, 0)),
            const((1, D_MODEL)),
            const((D_MODEL, D_POOL)),
            const((D_MODEL, D_ATTN)),
            const((2 * D_ATTN, D_MODEL)),
            const((len(POOL_WINDOWS), POOL_GROUP, POOL_GROUP)),
            const((1, D_POOL)),
        ],
        out_specs=[
            pl.BlockSpec((tm, D_ATTN), lambda i: (i, 0)),
            pl.BlockSpec((tm // MOBA_BLOCK, 2 * D_ATTN, MOBA_BLOCK), lambda i: (i, 0, 0)),
            pl.BlockSpec((tm, D_POOL), lambda i: (i, 0)),
        ],
        out_shape=[
            jax.ShapeDtypeStruct((t, D_ATTN), _BF16),
            jax.ShapeDtypeStruct((t // MOBA_BLOCK, 2 * D_ATTN, MOBA_BLOCK), _BF16),
            jax.ShapeDtypeStruct((t, D_POOL), _BF16),
        ],
        scratch_shapes=[pltpu.VMEM((POOL_HALO + tm, D_POOL), _F32)],
        compiler_params=pltpu.CompilerParams(
            dimension_semantics=("arbitrary",), vmem_limit_bytes=VMEM_LIMIT),
        name="proj_pool",
    )(x2d, g, wu, wk, wqv, wpool, pscale)


def _select_blocks(gate, n_past):
    blk = lax.broadcasted_iota(jnp.int32, gate.shape, 0)
    n_blocks = gate.shape[0]
    g = jnp.where(blk < n_past, gate, -jnp.inf)
    sel = jnp.zeros(gate.shape, _F32)
    for _ in range(MOBA_TOPK):
        mx = jnp.max(g, axis=0, keepdims=True)
        is_max = jnp.logical_and(g == mx, mx > -jnp.inf)
        first = jnp.min(jnp.where(is_max, blk, n_blocks), axis=0, keepdims=True)
        pick = blk == first
        sel = jnp.where(pick, 1.0, sel)
        g = jnp.where(pick, -jnp.inf, g)
    return sel


def _moba_kernel(q_ref, k_ref, v_ref, tab_ref, o_ref, kmean_ref, fsel_ref, s_ref, acc_ref):
    j = pl.program_id(1)
    nb = v_ref.shape[0]

    @pl.when(j == 0)
    def _():
        for n in range(nb):
            kb = k_ref[n * MOBA_BLOCK:(n + 1) * MOBA_BLOCK, :].astype(_F32)
            kmean_ref[n:n + 1, :] = jnp.sum(kb, axis=0, keepdims=True) * (1.0 / MOBA_BLOCK)

    tq = q_ref.shape[2]
    head = lambda a: slice(a * HEAD_DIM, (a + 1) * HEAD_DIM)
    q_heads = [q_ref[0, head(a), :] for a in range(HEADS_PER_STEP)]

    kmean = kmean_ref[...].astype(_BF16)
    blk = lax.broadcasted_iota(jnp.int32, (nb, tq), 0)
    adj_flag = []
    for a in range(HEADS_PER_STEP):
        sel = _select_blocks(_dot(kmean[:, head(a)], q_heads[a]), j)
        is_adj = blk == j - 1
        fsel_ref[a] = jnp.where(is_adj, 0.0, sel)
        flag = jnp.max(jnp.where(is_adj, sel, 0.0), axis=0, keepdims=True)
        adj_flag.append(jnp.where(j == 0, 1.0, flag))

    first = jnp.maximum(j - 1, 0)
    variant = jnp.where(j == 0, 1, 0)
    n_far_chunks = lax.shift_right_logical(j, 1)

    kb = k_ref[pl.ds(pl.multiple_of(first * MOBA_BLOCK, MOBA_BLOCK), KEY_CHUNK), :]
    cmax0 = []
    for a in range(HEADS_PER_STEP):
        s = _dot(kb[:, head(a)], q_heads[a]) + tab_ref[a, variant]
        top = jnp.where(adj_flag[a] > 0.0, s[:MOBA_BLOCK], MASKED)
        s = jnp.concatenate([top, s[MOBA_BLOCK:]], axis=0)
        s_ref[a] = s
        cmax0.append(jnp.max(s, axis=0, keepdims=True))
        acc_ref[a] = jnp.zeros((HEAD_DIM, tq), _F32)

    ones = jnp.ones((SUM_ROWS, MOBA_BLOCK), _BF16)

    def fold_head(a, first_block, cmax, m, l):
        m_new = jnp.maximum(m, cmax)
        alpha = jnp.exp2(m - m_new)
        p = jnp.exp2(s_ref[a] - m_new).astype(_BF16)
        pv = None
        for r in range(BLOCKS_PER_CHUNK):
            v_t = jnp.concatenate([v_ref[first_block + r, head(a), :], ones], axis=0)
            d = _dot(v_t, p[r * MOBA_BLOCK:(r + 1) * MOBA_BLOCK])
            pv = d if pv is None else pv + d
        acc_ref[a] = alpha * acc_ref[a] + pv[:HEAD_DIM]
        return m_new, alpha * l + pv[HEAD_DIM:HEAD_DIM + 1]

    def step(i, carry):
        cmax, m, l = carry
        kb = k_ref[pl.ds(pl.multiple_of(i * KEY_CHUNK, KEY_CHUNK), KEY_CHUNK), :]
        first_block = jnp.where(i == 0, first, (i - 1) * BLOCKS_PER_CHUNK)
        new_cmax, new_m, new_l = [], [], []
        score = _dot(kb[:, head(0)], q_heads[0])
        for a in range(HEADS_PER_STEP):
            next_score = None
            if a + 1 < HEADS_PER_STEP:
                next_score = _dot(kb[:, head(a + 1)], q_heads[a + 1])
            m_a, l_a = fold_head(a, first_block, cmax[a], m[a], l[a])
            parts = []
            for r in range(BLOCKS_PER_CHUNK):
                flag = fsel_ref[a, pl.ds(i * BLOCKS_PER_CHUNK + r, 1), :]
                part = score[r * MOBA_BLOCK:(r + 1) * MOBA_BLOCK]
                parts.append(jnp.where(flag > 0.0, part, MASKED))
            s = jnp.concatenate(parts, axis=0)
            s_ref[a] = s
            new_cmax.append(jnp.max(s, axis=0, keepdims=True))
            new_m.append(m_a)
            new_l.append(l_a)
            score = next_score
        return tuple(new_cmax), tuple(new_m), tuple(new_l)

    init_m = tuple(jnp.full((1, tq), MASKED, _F32) for _ in range(HEADS_PER_STEP))
    init_l = tuple(jnp.zeros((1, tq), _F32) for _ in range(HEADS_PER_STEP))
    cmax, m, l = lax.fori_loop(0, n_far_chunks, step, (tuple(cmax0), init_m, init_l))
    last_block = jnp.where(n_far_chunks == 0, first, (n_far_chunks - 1) * BLOCKS_PER_CHUNK)
    l = [fold_head(a, last_block, cmax[a], m[a], l[a])[1] for a in range(HEADS_PER_STEP)]

    out_t = jnp.concatenate([acc_ref[a] / l[a] for a in range(HEADS_PER_STEP)], axis=0)
    o_ref[...] = out_t.T.astype(_BF16)


def _moba(qv3, k, tables, *, batch, seq_len):
    nb = seq_len // MOBA_BLOCK
    t = batch * seq_len
    assert nb % BLOCKS_PER_CHUNK == 0 and HEADS_PER_STEP == N_HEADS
    return pl.pallas_call(
        _moba_kernel,
        grid=(batch, nb),
        in_specs=[
            pl.BlockSpec((1, D_ATTN, MOBA_BLOCK), lambda b, j: (b * nb + j, 0, 0)),
            pl.BlockSpec((seq_len, D_ATTN), lambda b, j: (b, 0)),
            pl.BlockSpec((nb, D_ATTN, MOBA_BLOCK), lambda b, j: (b, 1, 0)),
            pl.BlockSpec((N_HEADS, 2, KEY_CHUNK, MOBA_BLOCK), lambda b, j: (0, 0, 0, 0),
                         pipeline_mode=pl.Buffered(1)),
        ],
        out_specs=pl.BlockSpec((MOBA_BLOCK, D_ATTN), lambda b, j: (b * nb + j, 0)),
        out_shape=jax.ShapeDtypeStruct((t, D_ATTN), _BF16),
        scratch_shapes=[pltpu.VMEM((nb, D_ATTN), _F32),
                        pltpu.VMEM((HEADS_PER_STEP, nb, MOBA_BLOCK), _F32),
                        pltpu.VMEM((HEADS_PER_STEP, KEY_CHUNK, MOBA_BLOCK), _F32),
                        pltpu.VMEM((HEADS_PER_STEP, HEAD_DIM, MOBA_BLOCK), _F32)],
        compiler_params=pltpu.CompilerParams(
            dimension_semantics=("arbitrary", "arbitrary"),
            vmem_limit_bytes=VMEM_LIMIT),
        name="moba_attention",
    )(qv3, k, qv3, tables)


def _shift_rows(g, prev_rows, lag):
    rolled = pltpu.roll(g, lag, 0)
    rid = lax.broadcasted_iota(jnp.int32, prev_rows.shape, 0)
    head = jnp.where(rid < lag, pltpu.roll(prev_rows, lag, 0), rolled[:CONV_HALO])
    return jnp.concatenate([head, rolled[CONV_HALO:]], axis=0)


def _ffn_kernel(x_ref, p_ref, ypool_ref, yattn_ref, wo_pool_ref, wo_attn_ref, g_post_ref,
                g_pre_ref, wgate_ref, wval_ref, convw_ref, convb_ref, wdown_ref, g_ffn_post_ref,
                wple_ref, wplegate_ref, o_ref, carry_ref, act_ref, *, tiles_per_seq):
    i = pl.program_id(0)
    tm = x_ref.shape[0]

    @pl.when(i % tiles_per_seq == 0)
    def _():
        carry_ref[...] = jnp.zeros(carry_ref.shape, _F32)

    y = _dot(ypool_ref[...], wo_pool_ref[...]) + _dot(yattn_ref[...], wo_attn_ref[...])
    x1 = x_ref[...] + _rms(y, g_post_ref[...])
    h = _rms(x1, g_pre_ref[...]).astype(_BF16)

    for c in range(N_FF_CHUNKS):
        cols = slice(c * FF_CHUNK, (c + 1) * FF_CHUNK)
        gate = _dot(h, wgate_ref[c])
        val = _dot(h, wval_ref[c])
        prev = carry_ref[:, cols]
        conv = convb_ref[:, cols] + _shift_rows(gate, prev, 2) * convw_ref[0:1, cols]
        conv = conv + _shift_rows(gate, prev, 1) * convw_ref[1:2, cols]
        conv = conv + gate * convw_ref[2:3, cols]
        carry_ref[:, cols] = gate[tm - CONV_HALO:, :]
        act_ref[:, cols] = (jax.nn.gelu(conv, approximate=True) * val).astype(_BF16)

    x2 = x1 + _rms(_dot(act_ref[...], wdown_ref[...]), g_ffn_post_ref[...])
    ple = _dot(p_ref[...].astype(_BF16), wple_ref[...])
    gate_logit = _dot(x2.astype(_BF16), wplegate_ref[...])
    o_ref[...] = x2 + ple * (1.0 / (1.0 + jnp.exp(-gate_logit)))


def _ffn(x2d, p2d, ypool, yattn, wo_pool, wo_attn, g_post, g_pre, wgate, wval, convw, convb,
         wdown, g_ffn_post, wple, wplegate, *, seq_len):
    t = x2d.shape[0]
    tm = ROW_TILE
    rows = lambda width: pl.BlockSpec((tm, width), lambda i: (i, 0))
    const = lambda shape: pl.BlockSpec(shape, lambda i: (0,) * len(shape),
                                       pipeline_mode=pl.Buffered(1))
    return pl.pallas_call(
        functools.partial(_ffn_kernel, tiles_per_seq=seq_len // tm),
        grid=(t // tm,),
        in_specs=[
            rows(D_MODEL), rows(PLE_DIM), rows(D_POOL), rows(D_ATTN),
            const((D_POOL, D_MODEL)), const((D_ATTN, D_MODEL)), const((1, D_MODEL)),
            const((1, D_MODEL)),
            const((N_FF_CHUNKS, D_MODEL, FF_CHUNK)), const((N_FF_CHUNKS, D_MODEL, FF_CHUNK)),
            const((CONV_WIDTH, D_FF)), const((1, D_FF)),
            const((D_FF, D_MODEL)), const((1, D_MODEL)),
            const((PLE_DIM, D_MODEL)), const((D_MODEL, D_MODEL)),
        ],
        out_specs=rows(D_MODEL),
        out_shape=jax.ShapeDtypeStruct((t, D_MODEL), _F32),
        scratch_shapes=[pltpu.VMEM((CONV_HALO, D_FF), _F32), pltpu.VMEM((tm, D_FF), _BF16)],
        compiler_params=pltpu.CompilerParams(
            dimension_semantics=("arbitrary",), vmem_limit_bytes=VMEM_LIMIT),
        name="ffn_block",
    )(x2d, p2d, ypool, yattn, wo_pool, wo_attn, g_post, g_pre, wgate, wval, convw, convb,
      wdown, g_ffn_post, wple, wplegate)


def _chunk_cols(w):
    return w.reshape(w.shape[0], N_FF_CHUNKS, FF_CHUNK).transpose(1, 0, 2)


def kernel(x, p, rel_bias, g_mix_pre, g_mix_post, g_ffn_pre, g_ffn_post, w_in, w_pool,
           pool_scale, w_out, w_up, conv_w, conv_b, w_down, w_ple, w_ple_gate):
    batch, seq_len, d_model = x.shape
    depth = w_in.shape[0]
    assert d_model == D_MODEL and seq_len % ROW_TILE == 0 and ROW_TILE % MOBA_BLOCK == 0
    t = batch * seq_len
    row = lambda v: v.reshape(1, -1).astype(_F32)

    tables = _bias_tables(rel_bias)
    xf = x.reshape(t, d_model).astype(_F32)
    for i in range(depth):
        w_u, w_q, w_k, w_v = jnp.split(
            w_in[i], [D_POOL, D_POOL + D_ATTN, D_POOL + 2 * D_ATTN], axis=-1)
        wqv_t = jnp.concatenate([w_q, w_v], axis=-1).T.astype(_BF16)
        k, qv3, ypool = _proj_pool(
            xf, row(g_mix_pre[i]), w_u.astype(_BF16), w_k.astype(_BF16), wqv_t,
            w_pool[i].astype(_BF16), row(pool_scale[i]), seq_len=seq_len)
        yattn = _moba(qv3, k, tables, batch=batch, seq_len=seq_len)
        xf = _ffn(
            xf, p[i].reshape(t, PLE_DIM).astype(_F32), ypool, yattn,
            w_out[i, :D_POOL].astype(_BF16), w_out[i, D_POOL:].astype(_BF16),
            row(g_mix_post[i]), row(g_ffn_pre[i]),
            _chunk_cols(w_up[i, :, :D_FF]).astype(_BF16),
            _chunk_cols(w_up[i, :, D_FF:]).astype(_BF16),
            conv_w[i].astype(_F32), row(conv_b[i]),
            w_down[i].astype(_BF16),
            row(g_ffn_post[i]), w_ple[i].astype(_BF16), w_ple_gate[i].astype(_BF16),
            seq_len=seq_len)
    return xf.reshape(batch, seq_len, d_model).astype(x.dtype)
```

```python
import functools
import math

import numpy as np
import jax
import jax.numpy as jnp
from jax import lax
from jax.experimental import pallas as pl
from jax.experimental.pallas import tpu as pltpu

D_MODEL = 1024
PLE_DIM = 256
D_POOL = 512
POOL_WINDOWS = (2, 4, 8, 16)
POOL_GROUP = D_POOL // len(POOL_WINDOWS)
D_ATTN = D_MODEL - D_POOL
N_HEADS = 8
HEAD_DIM = D_ATTN // N_HEADS
MOBA_BLOCK = 256
MOBA_TOPK = 3
NUM_BUCKETS = 32
MAX_DISTANCE = 128
D_FF = 2816
CONV_WIDTH = 3
EPS = 1e-6

SEQS_PER_STEP = 2
POOL_HALO = 16
CONV_HALO = 8
FF_CHUNK = 256
N_FF_CHUNKS = D_FF // FF_CHUNK
ROW_TILE = 512
MASKED = -1e30
LOG2_E = math.log2(math.e)
BLOCKS_PER_CHUNK = 2
KEY_CHUNK = BLOCKS_PER_CHUNK * MOBA_BLOCK
SUM_ROWS = 16
VMEM_LIMIT = 56 * 1024 * 1024

_BF16 = jnp.bfloat16
_F32 = jnp.float32


def _rms(x, g):
    return x * lax.rsqrt(jnp.mean(x * x, axis=-1, keepdims=True) + EPS) * g


def _dot(a, b):
    return jnp.dot(a, b, preferred_element_type=_F32)


def _bucket_tables():
    key = np.arange(MOBA_BLOCK)[:, None]
    qry = np.arange(MOBA_BLOCK)[None, :]

    def bucket(dist):
        n = np.maximum(dist, 0)
        max_exact = NUM_BUCKETS // 2
        nf = np.maximum(n, 1).astype(np.float32)
        large = max_exact + (np.log(nf / max_exact) / math.log(MAX_DISTANCE / max_exact)
                             * (NUM_BUCKETS - max_exact)).astype(np.int32)
        large = np.minimum(large, NUM_BUCKETS - 1)
        return np.where(n < max_exact, n, large).astype(np.int32)

    d_own = qry - key
    own = np.where(d_own >= 0, bucket(d_own), -1).astype(np.int32)
    adj = bucket(d_own + MOBA_BLOCK)
    return own, adj


def _bias_table_kernel(rb_ref, own_idx_ref, adj_idx_ref, tab_ref):
    h = pl.program_id(0)
    far = rb_ref[NUM_BUCKETS - 1, h]
    own_idx = own_idx_ref[...]
    adj_idx = adj_idx_ref[...]
    own = jnp.zeros(own_idx.shape, _F32)
    adj = jnp.zeros(adj_idx.shape, _F32)
    for b in range(NUM_BUCKETS):
        val = (rb_ref[b, h] - far) * LOG2_E
        own = jnp.where(own_idx == b, val, own)
        adj = jnp.where(adj_idx == b, val, adj)
    own = jnp.where(own_idx < 0, MASKED, own)
    tab_ref[0, 0, 0:MOBA_BLOCK, :] = adj
    tab_ref[0, 0, MOBA_BLOCK:, :] = own
    tab_ref[0, 1, 0:MOBA_BLOCK, :] = own
    tab_ref[0, 1, MOBA_BLOCK:, :] = jnp.full(own.shape, MASKED, _F32)


def _bias_tables(rel_bias):
    own_idx, adj_idx = _bucket_tables()
    blk = pl.BlockSpec((MOBA_BLOCK, MOBA_BLOCK), lambda h: (0, 0))
    return pl.pallas_call(
        _bias_table_kernel,
        grid=(N_HEADS,),
        in_specs=[pl.BlockSpec(memory_space=pltpu.SMEM), blk, blk],
        out_specs=pl.BlockSpec((1, 2, KEY_CHUNK, MOBA_BLOCK), lambda h: (h, 0, 0, 0)),
        out_shape=jax.ShapeDtypeStruct((N_HEADS, 2, KEY_CHUNK, MOBA_BLOCK), _F32),
        name="bias_tables",
    )(rel_bias.astype(_F32), jnp.asarray(own_idx), jnp.asarray(adj_idx))


def _proj_pool_kernel(x_ref, g_ref, wu_ref, wk_ref, wqv_ref, wpool_ref, pscale_ref,
                      k_ref, qv_ref, ypool_ref, ext_ref, *, tiles_per_seq):
    i = pl.program_id(0)
    tm = x_ref.shape[0]
    h = _rms(x_ref[...], g_ref[...]).astype(_BF16)
    seq_tile = i % tiles_per_seq

    @pl.when(seq_tile == 0)
    def _():
        ext_ref[0:POOL_HALO, :] = jnp.zeros((POOL_HALO, D_POOL), _F32)

    u = _dot(h, wu_ref[...])
    ext_ref[POOL_HALO:POOL_HALO + tm, :] = u

    k_ref[...] = _dot(h, wk_ref[...]).astype(_BF16)

    qv = lax.dot_general(wqv_ref[...], h, (((1,), (1,)), ((), ())),
                         preferred_element_type=_F32)
    row = lax.broadcasted_iota(jnp.int32, qv.shape, 0)
    qv = jnp.where(row < D_ATTN, qv * (HEAD_DIM ** -0.5 * LOG2_E), qv).astype(_BF16)
    for c in range(tm // MOBA_BLOCK):
        qv_ref[c] = qv[:, c * MOBA_BLOCK:(c + 1) * MOBA_BLOCK]

    pos = seq_tile * tm + lax.broadcasted_iota(jnp.int32, (tm, 1), 0)
    for g, w in enumerate(POOL_WINDOWS):
        cols = slice(g * POOL_GROUP, (g + 1) * POOL_GROUP)
        u_g = u[:, cols]
        acc = u_g
        for lag in range(1, w):
            acc = acc + ext_ref[POOL_HALO - lag:POOL_HALO - lag + tm, cols]
        count = jnp.minimum(pos + 1, w).astype(_F32)
        pooled = (acc / count - u_g).astype(_BF16)
        mixed = _dot(pooled, wpool_ref[g]) * pscale_ref[:, cols]
        ypool_ref[:, cols] = mixed.astype(_BF16)
    ext_ref[0:POOL_HALO, :] = u[tm - POOL_HALO:, :]


def _proj_pool(x2d, g, wu, wk, wqv, wpool, pscale, *, seq_len):
    t = x2d.shape[0]
    tm = ROW_TILE
    const = lambda shape: pl.BlockSpec(shape, lambda i: (0,) * len(shape))
    return pl.pallas_call(
        functools.partial(_proj_pool_kernel, tiles_per_seq=seq_len // tm),
        grid=(t // tm,),
        in_specs=[
            pl.BlockSpec((tm, D_MODEL), lambda i: (i, 0)),
            const((1, D_MODEL)),
            const((D_MODEL, D_POOL)),
            const((D_MODEL, D_ATTN)),
            const((2 * D_ATTN, D_MODEL)),
            const((len(POOL_WINDOWS), POOL_GROUP, POOL_GROUP)),
            const((1, D_POOL)),
        ],
        out_specs=[
            pl.BlockSpec((tm, D_ATTN), lambda i: (i, 0)),
            pl.BlockSpec((tm // MOBA_BLOCK, 2 * D_ATTN, MOBA_BLOCK), lambda i: (i, 0, 0)),
            pl.BlockSpec((tm, D_POOL), lambda i: (i, 0)),
        ],
        out_shape=[
            jax.ShapeDtypeStruct((t, D_ATTN), _BF16),
            jax.ShapeDtypeStruct((t // MOBA_BLOCK, 2 * D_ATTN, MOBA_BLOCK), _BF16),
            jax.ShapeDtypeStruct((t, D_POOL), _BF16),
        ],
        scratch_shapes=[pltpu.VMEM((POOL_HALO + tm, D_POOL), _F32)],
        compiler_params=pltpu.CompilerParams(
            dimension_semantics=("arbitrary",), vmem_limit_bytes=VMEM_LIMIT),
        name="proj_pool",
    )(x2d, g, wu, wk, wqv, wpool, pscale)


def _select_blocks(gate, n_past):
    blk = lax.broadcasted_iota(jnp.int32, gate.shape, 0)
    n_blocks = gate.shape[0]
    g = jnp.where(blk < n_past, gate, -jnp.inf)
    sel = jnp.zeros(gate.shape, _F32)
    for _ in range(MOBA_TOPK):
        mx = jnp.max(g, axis=0, keepdims=True)
        is_max = jnp.logical_and(g == mx, mx > -jnp.inf)
        first = jnp.min(jnp.where(is_max, blk, n_blocks), axis=0, keepdims=True)
        pick = blk == first
        sel = jnp.where(pick, 1.0, sel)
        g = jnp.where(pick, -jnp.inf, g)
    return sel


def _moba_kernel(q_ref, k_ref, v_ref, tab_ref, o_ref, kmean_ref, fsel_ref, s_ref, acc_ref):
    j = pl.program_id(1)
    nb = v_ref.shape[1]
    lanes = [(sq, a) for sq in range(SEQS_PER_STEP) for a in range(N_HEADS)]
    n_lanes = len(lanes)

    @pl.when(j == 0)
    def _():
        for sq in range(SEQS_PER_STEP):
            for n in range(nb):
                kb = k_ref[sq, n * MOBA_BLOCK:(n + 1) * MOBA_BLOCK, :].astype(_F32)
                kmean_ref[sq, n:n + 1, :] = (
                    jnp.sum(kb, axis=0, keepdims=True) * (1.0 / MOBA_BLOCK))

    tq = q_ref.shape[3]
    head = lambda a: slice(a * HEAD_DIM, (a + 1) * HEAD_DIM)
    q_heads = [q_ref[sq, 0, head(a), :] for sq, a in lanes]

    kmean = [kmean_ref[sq].astype(_BF16) for sq in range(SEQS_PER_STEP)]
    blk = lax.broadcasted_iota(jnp.int32, (nb, tq), 0)
    adj_flag = []
    for c, (sq, a) in enumerate(lanes):
        sel = _select_blocks(_dot(kmean[sq][:, head(a)], q_heads[c]), j)
        is_adj = blk == j - 1
        fsel_ref[c] = jnp.where(is_adj, 0.0, sel)
        flag = jnp.max(jnp.where(is_adj, sel, 0.0), axis=0, keepdims=True)
        adj_flag.append(jnp.where(j == 0, 1.0, flag))

    first = jnp.maximum(j - 1, 0)
    variant = jnp.where(j == 0, 1, 0)
    n_far_chunks = lax.shift_right_logical(j, 1)

    def key_chunk(first_row):
        rows = pl.ds(pl.multiple_of(first_row, MOBA_BLOCK), KEY_CHUNK)
        return [k_ref[sq, rows, :] for sq in range(SEQS_PER_STEP)]

    kb = key_chunk(first * MOBA_BLOCK)
    cmax0 = []
    for c, (sq, a) in enumerate(lanes):
        s = _dot(kb[sq][:, head(a)], q_heads[c]) + tab_ref[a, variant]
        top = jnp.where(adj_flag[c] > 0.0, s[:MOBA_BLOCK], MASKED)
        s = jnp.concatenate([top, s[MOBA_BLOCK:]], axis=0)
        s_ref[c] = s
        cmax0.append(jnp.max(s, axis=0, keepdims=True))
        acc_ref[c] = jnp.zeros((HEAD_DIM, tq), _F32)

    ones = jnp.ones((SUM_ROWS, MOBA_BLOCK), _BF16)

    def fold_lane(c, first_block, cmax, m, l):
        sq, a = lanes[c]
        m_new = jnp.maximum(m, cmax)
        alpha = jnp.exp2(m - m_new)
        p = jnp.exp2(s_ref[c] - m_new).astype(_BF16)
        pv = None
        for r in range(BLOCKS_PER_CHUNK):
            v_t = jnp.concatenate([v_ref[sq, first_block + r, head(a), :], ones], axis=0)
            d = _dot(v_t, p[r * MOBA_BLOCK:(r + 1) * MOBA_BLOCK])
            pv = d if pv is None else pv + d
        acc_ref[c] = alpha * acc_ref[c] + pv[:HEAD_DIM]
        return m_new, alpha * l + pv[HEAD_DIM:HEAD_DIM + 1]

    def step(i, carry):
        cmax, m, l = carry
        kb = key_chunk(i * KEY_CHUNK)
        score_lane = lambda c: _dot(kb[lanes[c][0]][:, head(lanes[c][1])], q_heads[c])
        first_block = jnp.where(i == 0, first, (i - 1) * BLOCKS_PER_CHUNK)
        new_cmax, new_m, new_l = [], [], []
        score = score_lane(0)
        for c in range(n_lanes):
            next_score = score_lane(c + 1) if c + 1 < n_lanes else None
            m_c, l_c = fold_lane(c, first_block, cmax[c], m[c], l[c])
            parts = []
            for r in range(BLOCKS_PER_CHUNK):
                flag = fsel_ref[c, pl.ds(i * BLOCKS_PER_CHUNK + r, 1), :]
                part = score[r * MOBA_BLOCK:(r + 1) * MOBA_BLOCK]
                parts.append(jnp.where(flag > 0.0, part, MASKED))
            s = jnp.concatenate(parts, axis=0)
            s_ref[c] = s
            new_cmax.append(jnp.max(s, axis=0, keepdims=True))
            new_m.append(m_c)
            new_l.append(l_c)
            score = next_score
        return tuple(new_cmax), tuple(new_m), tuple(new_l)

    init_m = tuple(jnp.full((1, tq), MASKED, _F32) for _ in range(n_lanes))
    init_l = tuple(jnp.zeros((1, tq), _F32) for _ in range(n_lanes))
    cmax, m, l = lax.fori_loop(0, n_far_chunks, step, (tuple(cmax0), init_m, init_l))
    last_block = jnp.where(n_far_chunks == 0, first, (n_far_chunks - 1) * BLOCKS_PER_CHUNK)
    l = [fold_lane(c, last_block, cmax[c], m[c], l[c])[1] for c in range(n_lanes)]

    for sq in range(SEQS_PER_STEP):
        out_t = jnp.concatenate(
            [acc_ref[c] / l[c] for c in range(sq * N_HEADS, (sq + 1) * N_HEADS)], axis=0)
        o_ref[sq] = out_t.T.astype(_BF16)


def _moba(qv3, k, tables, *, batch, seq_len):
    nb = seq_len // MOBA_BLOCK
    sq = SEQS_PER_STEP
    n_lanes = sq * N_HEADS
    assert nb % BLOCKS_PER_CHUNK == 0 and batch % sq == 0
    qv4 = qv3.reshape(batch, nb, 2 * D_ATTN, MOBA_BLOCK)
    k3 = k.reshape(batch, seq_len, D_ATTN)
    resident = dict(pipeline_mode=pl.Buffered(1))
    out = pl.pallas_call(
        _moba_kernel,
        grid=(batch // sq, nb),
        in_specs=[
            pl.BlockSpec((sq, 1, D_ATTN, MOBA_BLOCK), lambda g, j: (g, j, 0, 0)),
            pl.BlockSpec((sq, seq_len, D_ATTN), lambda g, j: (g, 0, 0), **resident),
            pl.BlockSpec((sq, nb, D_ATTN, MOBA_BLOCK), lambda g, j: (g, 0, 1, 0), **resident),
            pl.BlockSpec((N_HEADS, 2, KEY_CHUNK, MOBA_BLOCK), lambda g, j: (0, 0, 0, 0),
                         **resident),
        ],
        out_specs=pl.BlockSpec((sq, MOBA_BLOCK, D_ATTN), lambda g, j: (g, j, 0)),
        out_shape=jax.ShapeDtypeStruct((batch, seq_len, D_ATTN), _BF16),
        scratch_shapes=[pltpu.VMEM((sq, nb, D_ATTN), _F32),
                        pltpu.VMEM((n_lanes, nb, MOBA_BLOCK), _F32),
                        pltpu.VMEM((n_lanes, KEY_CHUNK, MOBA_BLOCK), _F32),
                        pltpu.VMEM((n_lanes, HEAD_DIM, MOBA_BLOCK), _F32)],
        compiler_params=pltpu.CompilerParams(
            dimension_semantics=("arbitrary", "arbitrary"),
            vmem_limit_bytes=VMEM_LIMIT),
        name="moba_attention",
    )(qv4, k3, qv4, tables)
    return out.reshape(batch * seq_len, D_ATTN)


def _shift_rows(g, prev_rows, lag):
    rolled = pltpu.roll(g, lag, 0)
    rid = lax.broadcasted_iota(jnp.int32, prev_rows.shape, 0)
    head = jnp.where(rid < lag, pltpu.roll(prev_rows, lag, 0), rolled[:CONV_HALO])
    return jnp.concatenate([head, rolled[CONV_HALO:]], axis=0)


def _ffn_kernel(x_ref, p_ref, ypool_ref, yattn_ref, wo_pool_ref, wo_attn_ref, g_post_ref,
                g_pre_ref, wup_ref, convw_ref, convb_ref, wdown_ref, g_ffn_post_ref,
                wple_ref, wplegate_ref, o_ref, carry_ref, act_ref, *, tiles_per_seq):
    i = pl.program_id(0)
    tm = x_ref.shape[0]

    @pl.when(i % tiles_per_seq == 0)
    def _():
        carry_ref[...] = jnp.zeros(carry_ref.shape, _F32)

    y = _dot(ypool_ref[...], wo_pool_ref[...]) + _dot(yattn_ref[...], wo_attn_ref[...])
    x1 = x_ref[...] + _rms(y, g_post_ref[...])
    h = _rms(x1, g_pre_ref[...]).astype(_BF16)

    for c in range(N_FF_CHUNKS):
        cols = slice(c * FF_CHUNK, (c + 1) * FF_CHUNK)
        gate = _dot(h, wup_ref[:, cols])
        val = _dot(h, wup_ref[:, D_FF + c * FF_CHUNK:D_FF + (c + 1) * FF_CHUNK])
        prev = carry_ref[:, cols]
        conv = convb_ref[:, cols] + _shift_rows(gate, prev, 2) * convw_ref[0:1, cols]
        conv = conv + _shift_rows(gate, prev, 1) * convw_ref[1:2, cols]
        conv = conv + gate * convw_ref[2:3, cols]
        carry_ref[:, cols] = gate[tm - CONV_HALO:, :]
        act_ref[:, cols] = (jax.nn.gelu(conv, approximate=True) * val).astype(_BF16)

    x2 = x1 + _rms(_dot(act_ref[...], wdown_ref[...]), g_ffn_post_ref[...])
    ple = _dot(p_ref[...].astype(_BF16), wple_ref[...])
    gate_logit = _dot(x2.astype(_BF16), wplegate_ref[...])
    o_ref[...] = x2 + ple * (1.0 / (1.0 + jnp.exp(-gate_logit)))


def _ffn(x2d, p2d, ypool, yattn, wo_pool, wo_attn, g_post, g_pre, wup, convw, convb,
         wdown, g_ffn_post, wple, wplegate, *, seq_len):
    t = x2d.shape[0]
    tm = ROW_TILE
    rows = lambda width: pl.BlockSpec((tm, width), lambda i: (i, 0))
    const = lambda shape: pl.BlockSpec(shape, lambda i: (0,) * len(shape),
                                       pipeline_mode=pl.Buffered(1))
    return pl.pallas_call(
        functools.partial(_ffn_kernel, tiles_per_seq=seq_len // tm),
        grid=(t // tm,),
        in_specs=[
            rows(D_MODEL), rows(PLE_DIM), rows(D_POOL), rows(D_ATTN),
            const((D_POOL, D_MODEL)), const((D_ATTN, D_MODEL)), const((1, D_MODEL)),
            const((1, D_MODEL)),
            const((D_MODEL, 2 * D_FF)),
            const((CONV_WIDTH, D_FF)), const((1, D_FF)),
            const((D_FF, D_MODEL)), const((1, D_MODEL)),
            const((PLE_DIM, D_MODEL)), const((D_MODEL, D_MODEL)),
        ],
        out_specs=rows(D_MODEL),
        out_shape=jax.ShapeDtypeStruct((t, D_MODEL), _F32),
        scratch_shapes=[pltpu.VMEM((CONV_HALO, D_FF), _F32), pltpu.VMEM((tm, D_FF), _BF16)],
        compiler_params=pltpu.CompilerParams(
            dimension_semantics=("arbitrary",), vmem_limit_bytes=VMEM_LIMIT),
        name="ffn_block",
    )(x2d, p2d, ypool, yattn, wo_pool, wo_attn, g_post, g_pre, wup, convw, convb,
      wdown, g_ffn_post, wple, wplegate)


def kernel(x, p, rel_bias, g_mix_pre, g_mix_post, g_ffn_pre, g_ffn_post, w_in, w_pool,
           pool_scale, w_out, w_up, conv_w, conv_b, w_down, w_ple, w_ple_gate):
    batch, seq_len, d_model = x.shape
    depth = w_in.shape[0]
    assert d_model == D_MODEL and seq_len % ROW_TILE == 0 and ROW_TILE % MOBA_BLOCK == 0
    t = batch * seq_len
    row = lambda v: v.reshape(1, -1).astype(_F32)

    tables = _bias_tables(rel_bias)
    xf = x.reshape(t, d_model).astype(_F32)
    for i in range(depth):
        w_u, w_q, w_k, w_v = jnp.split(
            w_in[i], [D_POOL, D_POOL + D_ATTN, D_POOL + 2 * D_ATTN], axis=-1)
        wqv_t = jnp.concatenate([w_q, w_v], axis=-1).T.astype(_BF16)
        k, qv3, ypool = _proj_pool(
            xf, row(g_mix_pre[i]), w_u.astype(_BF16), w_k.astype(_BF16), wqv_t,
            w_pool[i].astype(_BF16), row(pool_scale[i]), seq_len=seq_len)
        yattn = _moba(qv3, k, tables, batch=batch, seq_len=seq_len)
        xf = _ffn(
            xf, p[i].reshape(t, PLE_DIM).astype(_F32), ypool, yattn,
            w_out[i, :D_POOL].astype(_BF16), w_out[i, D_POOL:].astype(_BF16),
            row(g_mix_post[i]), row(g_ffn_pre[i]),
            w_up[i].astype(_BF16),
            conv_w[i].astype(_F32), row(conv_b[i]),
            w_down[i].astype(_BF16),
            row(g_ffn_post[i]), w_ple[i].astype(_BF16), w_ple_gate[i].astype(_BF16),
            seq_len=seq_len)
    return xf.reshape(batch, seq_len, d_model).astype(x.dtype)
```

```python
import functools
import math

import numpy as np
import jax
import jax.numpy as jnp
from jax import lax
from jax.experimental import pallas as pl
from jax.experimental.pallas import tpu as pltpu

D_MODEL = 1024
PLE_DIM = 256
D_POOL = 512
POOL_WINDOWS = (2, 4, 8, 16)
POOL_GROUP = D_POOL // len(POOL_WINDOWS)
D_ATTN = D_MODEL - D_POOL
N_HEADS = 8
HEAD_DIM = D_ATTN // N_HEADS
MOBA_BLOCK = 256
MOBA_TOPK = 3
NUM_BUCKETS = 32
MAX_DISTANCE = 128
D_FF = 2816
CONV_WIDTH = 3
EPS = 1e-6

SEQS_PER_STEP = 2
POOL_HALO = 16
CONV_HALO = 8
FF_CHUNK = 256
N_FF_CHUNKS = D_FF // FF_CHUNK
ROW_TILE = 512
MASKED = -1e30
LOG2_E = math.log2(math.e)
BLOCKS_PER_CHUNK = 2
KEY_CHUNK = BLOCKS_PER_CHUNK * MOBA_BLOCK
SUM_ROWS = 16
VMEM_LIMIT = 56 * 1024 * 1024

_BF16 = jnp.bfloat16
_F32 = jnp.float32


def _rms(x, g):
    return x * lax.rsqrt(jnp.mean(x * x, axis=-1, keepdims=True) + EPS) * g


def _dot(a, b):
    return jnp.dot(a, b, preferred_element_type=_F32)


def _bucket_tables():
    key = np.arange(MOBA_BLOCK)[:, None]
    qry = np.arange(MOBA_BLOCK)[None, :]

    def bucket(dist):
        n = np.maximum(dist, 0)
        max_exact = NUM_BUCKETS // 2
        nf = np.maximum(n, 1).astype(np.float32)
        large = max_exact + (np.log(nf / max_exact) / math.log(MAX_DISTANCE / max_exact)
                             * (NUM_BUCKETS - max_exact)).astype(np.int32)
        large = np.minimum(large, NUM_BUCKETS - 1)
        return np.where(n < max_exact, n, large).astype(np.int32)

    d_own = qry - key
    own = np.where(d_own >= 0, bucket(d_own), -1).astype(np.int32)
    adj = bucket(d_own + MOBA_BLOCK)
    return own, adj


def _bias_table_kernel(rb_ref, own_idx_ref, adj_idx_ref, tab_ref):
    h = pl.program_id(0)
    far = rb_ref[NUM_BUCKETS - 1, h]
    own_idx = own_idx_ref[...]
    adj_idx = adj_idx_ref[...]
    own = jnp.zeros(own_idx.shape, _F32)
    adj = jnp.zeros(adj_idx.shape, _F32)
    for b in range(NUM_BUCKETS):
        val = (rb_ref[b, h] - far) * LOG2_E
        own = jnp.where(own_idx == b, val, own)
        adj = jnp.where(adj_idx == b, val, adj)
    own = jnp.where(own_idx < 0, MASKED, own)
    tab_ref[0, 0, 0:MOBA_BLOCK, :] = adj
    tab_ref[0, 0, MOBA_BLOCK:, :] = own
    tab_ref[0, 1, 0:MOBA_BLOCK, :] = own
    tab_ref[0, 1, MOBA_BLOCK:, :] = jnp.full(own.shape, MASKED, _F32)


def _bias_tables(rel_bias):
    own_idx, adj_idx = _bucket_tables()
    blk = pl.BlockSpec((MOBA_BLOCK, MOBA_BLOCK), lambda h: (0, 0))
    return pl.pallas_call(
        _bias_table_kernel,
        grid=(N_HEADS,),
        in_specs=[pl.BlockSpec(memory_space=pltpu.SMEM), blk, blk],
        out_specs=pl.BlockSpec((1, 2, KEY_CHUNK, MOBA_BLOCK), lambda h: (h, 0, 0, 0)),
        out_shape=jax.ShapeDtypeStruct((N_HEADS, 2, KEY_CHUNK, MOBA_BLOCK), _F32),
        name="bias_tables",
    )(rel_bias.astype(_F32), jnp.asarray(own_idx), jnp.asarray(adj_idx))


def _proj_pool_kernel(x_ref, g_ref, win_ref, wqv_ref, wpool_ref, pscale_ref,
                      k_ref, qv_ref, ypool_ref, ext_ref, *, tiles_per_seq):
    i = pl.program_id(0)
    tm = x_ref.shape[0]
    h = _rms(x_ref[...], g_ref[...]).astype(_BF16)
    seq_tile = i % tiles_per_seq

    @pl.when(seq_tile == 0)
    def _():
        ext_ref[0:POOL_HALO, :] = jnp.zeros((POOL_HALO, D_POOL), _F32)

    u = _dot(h, win_ref[:, 0:D_POOL])
    ext_ref[POOL_HALO:POOL_HALO + tm, :] = u

    k_ref[...] = _dot(h, win_ref[:, D_POOL + D_ATTN:D_POOL + 2 * D_ATTN]).astype(_BF16)

    qv = lax.dot_general(wqv_ref[...], h, (((1,), (1,)), ((), ())),
                         preferred_element_type=_F32)
    row = lax.broadcasted_iota(jnp.int32, qv.shape, 0)
    qv = jnp.where(row < D_ATTN, qv * (HEAD_DIM ** -0.5 * LOG2_E), qv).astype(_BF16)
    for c in range(tm // MOBA_BLOCK):
        qv_ref[c] = qv[:, c * MOBA_BLOCK:(c + 1) * MOBA_BLOCK]

    pos = seq_tile * tm + lax.broadcasted_iota(jnp.int32, (tm, 1), 0)
    for g, w in enumerate(POOL_WINDOWS):
        cols = slice(g * POOL_GROUP, (g + 1) * POOL_GROUP)
        u_g = u[:, cols]
        acc = u_g
        for lag in range(1, w):
            acc = acc + ext_ref[POOL_HALO - lag:POOL_HALO - lag + tm, cols]
        count = jnp.minimum(pos + 1, w).astype(_F32)
        pooled = (acc / count - u_g).astype(_BF16)
        mixed = _dot(pooled, wpool_ref[g]) * pscale_ref[:, cols]
        ypool_ref[:, cols] = mixed.astype(_BF16)
    ext_ref[0:POOL_HALO, :] = u[tm - POOL_HALO:, :]


def _layer_block(layer, shape, **kwargs):
    return pl.BlockSpec((None,) + shape, lambda i: (layer,) + (0,) * len(shape), **kwargs)


def _proj_pool(x2d, g, win, wqv, wpool, pscale, *, layer, seq_len):
    t = x2d.shape[0]
    tm = ROW_TILE
    const = functools.partial(_layer_block, layer)
    return pl.pallas_call(
        functools.partial(_proj_pool_kernel, tiles_per_seq=seq_len // tm),
        grid=(t // tm,),
        in_specs=[
            pl.BlockSpec((tm, D_MODEL), lambda i: (i, 0)),
            const((1, D_MODEL)),
            const((D_MODEL, D_POOL + 3 * D_ATTN)),
            const((2 * D_ATTN, D_MODEL)),
            const((len(POOL_WINDOWS), POOL_GROUP, POOL_GROUP)),
            const((1, D_POOL)),
        ],
        out_specs=[
            pl.BlockSpec((tm, D_ATTN), lambda i: (i, 0)),
            pl.BlockSpec((tm // MOBA_BLOCK, 2 * D_ATTN, MOBA_BLOCK), lambda i: (i, 0, 0)),
            pl.BlockSpec((tm, D_POOL), lambda i: (i, 0)),
        ],
        out_shape=[
            jax.ShapeDtypeStruct((t, D_ATTN), _BF16),
            jax.ShapeDtypeStruct((t // MOBA_BLOCK, 2 * D_ATTN, MOBA_BLOCK), _BF16),
            jax.ShapeDtypeStruct((t, D_POOL), _BF16),
        ],
        scratch_shapes=[pltpu.VMEM((POOL_HALO + tm, D_POOL), _F32)],
        compiler_params=pltpu.CompilerParams(
            dimension_semantics=("arbitrary",), vmem_limit_bytes=VMEM_LIMIT),
        name="proj_pool",
    )(x2d, g, win, wqv, wpool, pscale)


def _select_blocks(gate, n_past):
    blk = lax.broadcasted_iota(jnp.int32, gate.shape, 0)
    n_blocks = gate.shape[0]
    g = jnp.where(blk < n_past, gate, -jnp.inf)
    sel = jnp.zeros(gate.shape, _F32)
    for _ in range(MOBA_TOPK):
        mx = jnp.max(g, axis=0, keepdims=True)
        is_max = jnp.logical_and(g == mx, mx > -jnp.inf)
        first = jnp.min(jnp.where(is_max, blk, n_blocks), axis=0, keepdims=True)
        pick = blk == first
        sel = jnp.where(pick, 1.0, sel)
        g = jnp.where(pick, -jnp.inf, g)
    return sel


def _moba_kernel(q_ref, k_ref, v_ref, tab_ref, o_ref, kmean_ref, fsel_ref, s_ref, acc_ref,
                 st_ref):
    t = pl.program_id(1)
    nb = v_ref.shape[1]
    tq = q_ref.shape[3]
    lanes = [(sq, a) for sq in range(SEQS_PER_STEP) for a in range(N_HEADS)]
    n_lanes = len(lanes)
    head = lambda a: slice(a * HEAD_DIM, (a + 1) * HEAD_DIM)
    CMAX, RUN_MAX, RUN_SUM = 0, 1, 2

    @pl.when(t == 0)
    def _():
        for sq in range(SEQS_PER_STEP):
            for n in range(nb):
                kb = k_ref[sq, n * MOBA_BLOCK:(n + 1) * MOBA_BLOCK, :].astype(_F32)
                kmean_ref[sq, n:n + 1, :] = (
                    jnp.sum(kb, axis=0, keepdims=True) * (1.0 / MOBA_BLOCK))
        s_ref[...] = jnp.full(s_ref.shape, MASKED, _F32)
        acc_ref[...] = jnp.zeros(acc_ref.shape, _F32)
        st_ref[CMAX] = jnp.full(st_ref.shape[1:], MASKED, _F32)
        st_ref[RUN_MAX] = jnp.full(st_ref.shape[1:], MASKED, _F32)
        st_ref[RUN_SUM] = jnp.ones(st_ref.shape[1:], _F32)

    j = jnp.minimum(t, nb - 1)
    prev = jnp.maximum(t - 1, 0)
    q_heads = [q_ref[sq, 0, head(a), :] for sq, a in lanes]

    kmean = [kmean_ref[sq].astype(_BF16) for sq in range(SEQS_PER_STEP)]
    blk = lax.broadcasted_iota(jnp.int32, (nb, tq), 0)
    adj_flag = []
    for c, (sq, a) in enumerate(lanes):
        sel = _select_blocks(_dot(kmean[sq][:, head(a)], q_heads[c]), j)
        is_adj = blk == j - 1
        fsel_ref[c] = jnp.where(is_adj, 0.0, sel)
        flag = jnp.max(jnp.where(is_adj, sel, 0.0), axis=0, keepdims=True)
        adj_flag.append(jnp.where(j == 0, 1.0, flag))

    first = jnp.maximum(j - 1, 0)
    variant = jnp.where(j == 0, 1, 0)
    n_far_chunks = jnp.where(t < nb, lax.shift_right_logical(j, 1), 0)
    prev_far_chunks = lax.shift_right_logical(prev, 1)
    prev_last_block = jnp.where(prev_far_chunks == 0, jnp.maximum(prev - 1, 0),
                                (prev_far_chunks - 1) * BLOCKS_PER_CHUNK)

    def key_chunk(first_row):
        rows = pl.ds(pl.multiple_of(first_row, MOBA_BLOCK), KEY_CHUNK)
        return [k_ref[sq, rows, :] for sq in range(SEQS_PER_STEP)]

    ones = jnp.ones((SUM_ROWS, MOBA_BLOCK), _BF16)

    def fold_lane(c, first_block, cmax, m, l):
        sq, a = lanes[c]
        m_new = jnp.maximum(m, cmax)
        alpha = jnp.exp2(m - m_new)
        p = jnp.exp2(s_ref[c] - m_new).astype(_BF16)
        pv = None
        for r in range(BLOCKS_PER_CHUNK):
            v_t = jnp.concatenate([v_ref[sq, first_block + r, head(a), :], ones], axis=0)
            d = _dot(v_t, p[r * MOBA_BLOCK:(r + 1) * MOBA_BLOCK])
            pv = d if pv is None else pv + d
        acc_ref[c] = alpha * acc_ref[c] + pv[:HEAD_DIM]
        return m_new, alpha * l + pv[HEAD_DIM:HEAD_DIM + 1]


    kb = key_chunk(first * MOBA_BLOCK)
    own_score = lambda c: _dot(kb[lanes[c][0]][:, head(lanes[c][1])], q_heads[c])
    score = own_score(0)
    cmax0, outs = [], []
    for c, (sq, a) in enumerate(lanes):
        next_score = own_score(c + 1) if c + 1 < n_lanes else None
        _, l_c = fold_lane(c, prev_last_block, st_ref[CMAX, c], st_ref[RUN_MAX, c],
                           st_ref[RUN_SUM, c])
        outs.append(acc_ref[c] / l_c)
        acc_ref[c] = jnp.zeros((HEAD_DIM, tq), _F32)
        s = score + tab_ref[a, variant]
        top = jnp.where(adj_flag[c] > 0.0, s[:MOBA_BLOCK], MASKED)
        s = jnp.concatenate([top, s[MOBA_BLOCK:]], axis=0)
        s_ref[c] = s
        cmax0.append(jnp.max(s, axis=0, keepdims=True))
        score = next_score
    for sq in range(SEQS_PER_STEP):
        out_t = jnp.concatenate(outs[sq * N_HEADS:(sq + 1) * N_HEADS], axis=0)
        o_ref[sq] = out_t.T.astype(_BF16)

    def step(i, carry):
        cmax, m, l = carry
        kb = key_chunk(i * KEY_CHUNK)
        score_lane = lambda c: _dot(kb[lanes[c][0]][:, head(lanes[c][1])], q_heads[c])
        first_block = jnp.where(i == 0, first, (i - 1) * BLOCKS_PER_CHUNK)
        new_cmax, new_m, new_l = [], [], []
        score = score_lane(0)
        for c in range(n_lanes):
            next_score = score_lane(c + 1) if c + 1 < n_lanes else None
            m_c, l_c = fold_lane(c, first_block, cmax[c], m[c], l[c])
            parts = []
            for r in range(BLOCKS_PER_CHUNK):
                flag = fsel_ref[c, pl.ds(i * BLOCKS_PER_CHUNK + r, 1), :]
                part = score[r * MOBA_BLOCK:(r + 1) * MOBA_BLOCK]
                parts.append(jnp.where(flag > 0.0, part, MASKED))
            s = jnp.concatenate(parts, axis=0)
            s_ref[c] = s
            new_cmax.append(jnp.max(s, axis=0, keepdims=True))
            new_m.append(m_c)
            new_l.append(l_c)
            score = next_score
        return tuple(new_cmax), tuple(new_m), tuple(new_l)

    init_m = tuple(jnp.full((1, tq), MASKED, _F32) for _ in range(n_lanes))
    init_l = tuple(jnp.zeros((1, tq), _F32) for _ in range(n_lanes))
    cmax, m, l = lax.fori_loop(0, n_far_chunks, step, (tuple(cmax0), init_m, init_l))
    for c in range(n_lanes):
        st_ref[CMAX, c] = cmax[c]
        st_ref[RUN_MAX, c] = m[c]
        st_ref[RUN_SUM, c] = l[c]


def _moba(qv3, k, tables, *, batch, seq_len):
    nb = seq_len // MOBA_BLOCK
    sq = SEQS_PER_STEP
    n_lanes = sq * N_HEADS
    assert nb % BLOCKS_PER_CHUNK == 0 and batch % sq == 0
    qv4 = qv3.reshape(batch, nb, 2 * D_ATTN, MOBA_BLOCK)
    k3 = k.reshape(batch, seq_len, D_ATTN)
    resident = dict(pipeline_mode=pl.Buffered(1))
    out = pl.pallas_call(
        _moba_kernel,
        grid=(batch // sq, nb + 1),
        in_specs=[
            pl.BlockSpec((sq, 1, D_ATTN, MOBA_BLOCK),
                         lambda g, t: (g, jnp.minimum(t, nb - 1), 0, 0)),
            pl.BlockSpec((sq, seq_len, D_ATTN), lambda g, t: (g, 0, 0), **resident),
            pl.BlockSpec((sq, nb, D_ATTN, MOBA_BLOCK), lambda g, t: (g, 0, 1, 0), **resident),
            pl.BlockSpec((N_HEADS, 2, KEY_CHUNK, MOBA_BLOCK), lambda g, t: (0, 0, 0, 0),
                         **resident),
        ],
        out_specs=pl.BlockSpec((sq, MOBA_BLOCK, D_ATTN),
                               lambda g, t: (g, jnp.maximum(t - 1, 0), 0)),
        out_shape=jax.ShapeDtypeStruct((batch, seq_len, D_ATTN), _BF16),
        scratch_shapes=[pltpu.VMEM((sq, nb, D_ATTN), _F32),
                        pltpu.VMEM((n_lanes, nb, MOBA_BLOCK), _F32),
                        pltpu.VMEM((n_lanes, KEY_CHUNK, MOBA_BLOCK), _F32),
                        pltpu.VMEM((n_lanes, HEAD_DIM, MOBA_BLOCK), _F32),
                        pltpu.VMEM((3, n_lanes, 1, MOBA_BLOCK), _F32)],
        compiler_params=pltpu.CompilerParams(
            dimension_semantics=("arbitrary", "arbitrary"),
            vmem_limit_bytes=VMEM_LIMIT),
        name="moba_attention",
    )(qv4, k3, qv4, tables)
    return out.reshape(batch * seq_len, D_ATTN)


def _shift_rows(g, prev_rows, lag):
    rolled = pltpu.roll(g, lag, 0)
    rid = lax.broadcasted_iota(jnp.int32, prev_rows.shape, 0)
    head = jnp.where(rid < lag, pltpu.roll(prev_rows, lag, 0), rolled[:CONV_HALO])
    return jnp.concatenate([head, rolled[CONV_HALO:]], axis=0)


def _ffn_kernel(x_ref, p_ref, ypool_ref, yattn_ref, wo_ref, g_post_ref,
                g_pre_ref, wup_ref, convw_ref, convb_ref, wdown_ref, g_ffn_post_ref,
                wple_ref, wplegate_ref, o_ref, carry_ref, act_ref, *, tiles_per_seq):
    i = pl.program_id(0)
    tm = x_ref.shape[0]

    @pl.when(i % tiles_per_seq == 0)
    def _():
        carry_ref[...] = jnp.zeros(carry_ref.shape, _F32)

    y = _dot(ypool_ref[...], wo_ref[0:D_POOL, :]) + _dot(yattn_ref[...], wo_ref[D_POOL:, :])
    x1 = x_ref[...] + _rms(y, g_post_ref[...])
    h = _rms(x1, g_pre_ref[...]).astype(_BF16)

    for c in range(N_FF_CHUNKS):
        cols = slice(c * FF_CHUNK, (c + 1) * FF_CHUNK)
        gate = _dot(h, wup_ref[:, cols])
        val = _dot(h, wup_ref[:, D_FF + c * FF_CHUNK:D_FF + (c + 1) * FF_CHUNK])
        prev = carry_ref[:, cols]
        conv = convb_ref[:, cols] + _shift_rows(gate, prev, 2) * convw_ref[0:1, cols]
        conv = conv + _shift_rows(gate, prev, 1) * convw_ref[1:2, cols]
        conv = conv + gate * convw_ref[2:3, cols]
        carry_ref[:, cols] = gate[tm - CONV_HALO:, :]
        act_ref[:, cols] = (jax.nn.gelu(conv, approximate=True) * val).astype(_BF16)

    x2 = x1 + _rms(_dot(act_ref[...], wdown_ref[...]), g_ffn_post_ref[...])
    ple = _dot(p_ref[...].astype(_BF16), wple_ref[...])
    gate_logit = _dot(x2.astype(_BF16), wplegate_ref[...])
    o_ref[...] = x2 + ple * (1.0 / (1.0 + jnp.exp(-gate_logit)))


def _ffn(x2d, p3d, ypool, yattn, wo, g_post, g_pre, wup, convw, convb,
         wdown, g_ffn_post, wple, wplegate, *, layer, seq_len):
    t = x2d.shape[0]
    tm = ROW_TILE
    rows = lambda width: pl.BlockSpec((tm, width), lambda i: (i, 0))
    const = functools.partial(_layer_block, layer, pipeline_mode=pl.Buffered(1))
    return pl.pallas_call(
        functools.partial(_ffn_kernel, tiles_per_seq=seq_len // tm),
        grid=(t // tm,),
        in_specs=[
            rows(D_MODEL), pl.BlockSpec((None, tm, PLE_DIM), lambda i: (layer, i, 0)),
            rows(D_POOL), rows(D_ATTN),
            const((D_MODEL, D_MODEL)), const((1, D_MODEL)),
            const((1, D_MODEL)),
            const((D_MODEL, 2 * D_FF)),
            const((CONV_WIDTH, D_FF)), const((1, D_FF)),
            const((D_FF, D_MODEL)), const((1, D_MODEL)),
            const((PLE_DIM, D_MODEL)), const((D_MODEL, D_MODEL)),
        ],
        out_specs=rows(D_MODEL),
        out_shape=jax.ShapeDtypeStruct((t, D_MODEL), _F32),
        scratch_shapes=[pltpu.VMEM((CONV_HALO, D_FF), _F32), pltpu.VMEM((tm, D_FF), _BF16)],
        compiler_params=pltpu.CompilerParams(
            dimension_semantics=("arbitrary",), vmem_limit_bytes=VMEM_LIMIT),
        name="ffn_block",
    )(x2d, p3d, ypool, yattn, wo, g_post, g_pre, wup, convw, convb,
      wdown, g_ffn_post, wple, wplegate)


def kernel(x, p, rel_bias, g_mix_pre, g_mix_post, g_ffn_pre, g_ffn_post, w_in, w_pool,
           pool_scale, w_out, w_up, conv_w, conv_b, w_down, w_ple, w_ple_gate):
    batch, seq_len, d_model = x.shape
    depth = w_in.shape[0]
    assert d_model == D_MODEL and seq_len % ROW_TILE == 0 and ROW_TILE % MOBA_BLOCK == 0
    t = batch * seq_len
    rows = lambda v: v.reshape(depth, 1, -1).astype(_F32)
    bf16 = lambda w: w.astype(_BF16)
    w_qv_t = jnp.concatenate(
        [w_in[..., D_POOL:D_POOL + D_ATTN], w_in[..., D_POOL + 2 * D_ATTN:]], axis=-1,
    ).astype(_BF16).transpose(0, 2, 1)
    w_in_b, w_pool_b, w_out_b, w_up_b, w_down_b = map(bf16, (w_in, w_pool, w_out, w_up, w_down))
    w_ple_b, w_ple_gate_b = bf16(w_ple), bf16(w_ple_gate)
    p3d = p.reshape(depth, t, PLE_DIM).astype(_F32)

    tables = _bias_tables(rel_bias)
    xf = x.reshape(t, d_model).astype(_F32)
    for i in range(depth):
        k, qv3, ypool = _proj_pool(
            xf, rows(g_mix_pre), w_in_b, w_qv_t, w_pool_b, rows(pool_scale),
            layer=i, seq_len=seq_len)
        yattn = _moba(qv3, k, tables, batch=batch, seq_len=seq_len)
        xf = _ffn(
            xf, p3d, ypool, yattn, w_out_b, rows(g_mix_post), rows(g_ffn_pre), w_up_b,
            conv_w.astype(_F32), rows(conv_b), w_down_b, rows(g_ffn_post), w_ple_b,
            w_ple_gate_b, layer=i, seq_len=seq_len)
    return xf.reshape(batch, seq_len, d_model).astype(x.dtype)
```

```python
import functools
import math

import numpy as np
import jax
import jax.numpy as jnp
from jax import lax
from jax.experimental import pallas as pl
from jax.experimental.pallas import tpu as pltpu

D_MODEL = 1024
PLE_DIM = 256
D_POOL = 512
POOL_WINDOWS = (2, 4, 8, 16)
POOL_GROUP = D_POOL // len(POOL_WINDOWS)
D_ATTN = D_MODEL - D_POOL
N_HEADS = 8
HEAD_DIM = D_ATTN // N_HEADS
MOBA_BLOCK = 256
MOBA_TOPK = 3
NUM_BUCKETS = 32
MAX_DISTANCE = 128
D_FF = 2816
CONV_WIDTH = 3
EPS = 1e-6

SEQS_PER_STEP = 2
POOL_HALO = 16
CONV_HALO = 8
FF_CHUNK = 256
N_FF_CHUNKS = D_FF // FF_CHUNK
ROW_TILE = 512
MASKED = -1e30
LOG2_E = math.log2(math.e)
BLOCKS_PER_CHUNK = 2
KEY_CHUNK = BLOCKS_PER_CHUNK * MOBA_BLOCK
SUM_ROWS = 16
VMEM_LIMIT = 56 * 1024 * 1024

_BF16 = jnp.bfloat16
_F32 = jnp.float32


def _rms(x, g):
    return x * lax.rsqrt(jnp.mean(x * x, axis=-1, keepdims=True) + EPS) * g


def _dot(a, b):
    return jnp.dot(a, b, preferred_element_type=_F32)


def _bucket_tables():
    key = np.arange(MOBA_BLOCK)[:, None]
    qry = np.arange(MOBA_BLOCK)[None, :]

    def bucket(dist):
        n = np.maximum(dist, 0)
        max_exact = NUM_BUCKETS // 2
        nf = np.maximum(n, 1).astype(np.float32)
        large = max_exact + (np.log(nf / max_exact) / math.log(MAX_DISTANCE / max_exact)
                             * (NUM_BUCKETS - max_exact)).astype(np.int32)
        large = np.minimum(large, NUM_BUCKETS - 1)
        return np.where(n < max_exact, n, large).astype(np.int32)

    d_own = qry - key
    own = np.where(d_own >= 0, bucket(d_own), -1).astype(np.int32)
    adj = bucket(d_own + MOBA_BLOCK)
    return own, adj


def _bias_table_kernel(rb_ref, own_idx_ref, adj_idx_ref, tab_ref):
    h = pl.program_id(0)
    far = rb_ref[NUM_BUCKETS - 1, h]
    own_idx = own_idx_ref[...]
    adj_idx = adj_idx_ref[...]
    own = jnp.zeros(own_idx.shape, _F32)
    adj = jnp.zeros(adj_idx.shape, _F32)
    for b in range(NUM_BUCKETS):
        val = (rb_ref[b, h] - far) * LOG2_E
        own = jnp.where(own_idx == b, val, own)
        adj = jnp.where(adj_idx == b, val, adj)
    own = jnp.where(own_idx < 0, MASKED, own)
    tab_ref[0, 0, 0:MOBA_BLOCK, :] = adj
    tab_ref[0, 0, MOBA_BLOCK:, :] = own
    tab_ref[0, 1, 0:MOBA_BLOCK, :] = own
    tab_ref[0, 1, MOBA_BLOCK:, :] = jnp.full(own.shape, MASKED, _F32)


def _bias_tables(rel_bias):
    own_idx, adj_idx = _bucket_tables()
    blk = pl.BlockSpec((MOBA_BLOCK, MOBA_BLOCK), lambda h: (0, 0))
    return pl.pallas_call(
        _bias_table_kernel,
        grid=(N_HEADS,),
        in_specs=[pl.BlockSpec(memory_space=pltpu.SMEM), blk, blk],
        out_specs=pl.BlockSpec((1, 2, KEY_CHUNK, MOBA_BLOCK), lambda h: (h, 0, 0, 0)),
        out_shape=jax.ShapeDtypeStruct((N_HEADS, 2, KEY_CHUNK, MOBA_BLOCK), _F32),
        name="bias_tables",
    )(rel_bias.astype(_F32), jnp.asarray(own_idx), jnp.asarray(adj_idx))


def _proj_pool_kernel(x_ref, g_ref, win_ref, wqv_ref, wpool_ref, pscale_ref,
                      k_ref, qv_ref, ypool_ref, ext_ref, *, tiles_per_seq):
    i = pl.program_id(0)
    tm = x_ref.shape[0]
    h = _rms(x_ref[...], g_ref[...]).astype(_BF16)
    seq_tile = i % tiles_per_seq

    @pl.when(seq_tile == 0)
    def _():
        ext_ref[0:POOL_HALO, :] = jnp.zeros((POOL_HALO, D_POOL), _F32)

    u = _dot(h, win_ref[:, 0:D_POOL])
    ext_ref[POOL_HALO:POOL_HALO + tm, :] = u

    nt_dims = (((1,), (1,)), ((), ()))
    q_t = lax.dot_general(wqv_ref[0:D_ATTN, :], h, nt_dims, preferred_element_type=_F32)
    q_t = (q_t * (HEAD_DIM ** -0.5 * LOG2_E)).astype(_BF16)
    v_t = lax.dot_general(wqv_ref[D_ATTN:, :], h, nt_dims,
                          preferred_element_type=_F32).astype(_BF16)
    for c in range(tm // MOBA_BLOCK):
        cols = slice(c * MOBA_BLOCK, (c + 1) * MOBA_BLOCK)
        qv_ref[c, 0:D_ATTN, :] = q_t[:, cols]
        qv_ref[c, D_ATTN:, :] = v_t[:, cols]

    k_ref[...] = _dot(h, win_ref[:, D_POOL + D_ATTN:D_POOL + 2 * D_ATTN]).astype(_BF16)

    pos = seq_tile * tm + lax.broadcasted_iota(jnp.int32, (tm, 1), 0)
    for g, w in enumerate(POOL_WINDOWS):
        cols = slice(g * POOL_GROUP, (g + 1) * POOL_GROUP)
        u_g = u[:, cols]
        acc = u_g
        for lag in range(1, w):
            acc = acc + ext_ref[POOL_HALO - lag:POOL_HALO - lag + tm, cols]
        count = jnp.minimum(pos + 1, w).astype(_F32)
        pooled = (acc / count - u_g).astype(_BF16)
        mixed = _dot(pooled, wpool_ref[g]) * pscale_ref[:, cols]
        ypool_ref[:, cols] = mixed.astype(_BF16)
    ext_ref[0:POOL_HALO, :] = u[tm - POOL_HALO:, :]


def _layer_block(layer, shape, **kwargs):
    return pl.BlockSpec((None,) + shape, lambda i: (layer,) + (0,) * len(shape), **kwargs)


def _proj_pool(x2d, g, win, wqv, wpool, pscale, *, layer, seq_len):
    t = x2d.shape[0]
    tm = ROW_TILE
    const = functools.partial(_layer_block, layer)
    return pl.pallas_call(
        functools.partial(_proj_pool_kernel, tiles_per_seq=seq_len // tm),
        grid=(t // tm,),
        in_specs=[
            pl.BlockSpec((tm, D_MODEL), lambda i: (i, 0)),
            const((1, D_MODEL)),
            const((D_MODEL, D_POOL + 3 * D_ATTN)),
            const((2 * D_ATTN, D_MODEL)),
            const((len(POOL_WINDOWS), POOL_GROUP, POOL_GROUP)),
            const((1, D_POOL)),
        ],
        out_specs=[
            pl.BlockSpec((tm, D_ATTN), lambda i: (i, 0)),
            pl.BlockSpec((tm // MOBA_BLOCK, 2 * D_ATTN, MOBA_BLOCK), lambda i: (i, 0, 0)),
            pl.BlockSpec((tm, D_POOL), lambda i: (i, 0)),
        ],
        out_shape=[
            jax.ShapeDtypeStruct((t, D_ATTN), _BF16),
            jax.ShapeDtypeStruct((t // MOBA_BLOCK, 2 * D_ATTN, MOBA_BLOCK), _BF16),
            jax.ShapeDtypeStruct((t, D_POOL), _BF16),
        ],
        scratch_shapes=[pltpu.VMEM((POOL_HALO + tm, D_POOL), _F32)],
        compiler_params=pltpu.CompilerParams(
            dimension_semantics=("arbitrary",), vmem_limit_bytes=VMEM_LIMIT),
        name="proj_pool",
    )(x2d, g, win, wqv, wpool, pscale)


def _select_blocks(gate, n_past):
    blk = lax.broadcasted_iota(jnp.int32, gate.shape, 0)
    n_blocks = gate.shape[0]
    g = jnp.where(blk < n_past, gate, -jnp.inf)
    sel = jnp.zeros(gate.shape, _F32)
    for _ in range(MOBA_TOPK):
        mx = jnp.max(g, axis=0, keepdims=True)
        is_max = jnp.logical_and(g == mx, mx > -jnp.inf)
        first = jnp.min(jnp.where(is_max, blk, n_blocks), axis=0, keepdims=True)
        pick = blk == first
        sel = jnp.where(pick, 1.0, sel)
        g = jnp.where(pick, -jnp.inf, g)
    return sel


def _moba_kernel(q_ref, k_ref, v_ref, tab_ref, o_ref, kmean_ref, fsel_ref, s_ref, acc_ref,
                 st_ref):
    t = pl.program_id(1)
    nb = v_ref.shape[1]
    tq = q_ref.shape[3]
    lanes = [(sq, a) for sq in range(SEQS_PER_STEP) for a in range(N_HEADS)]
    n_lanes = len(lanes)
    head = lambda a: slice(a * HEAD_DIM, (a + 1) * HEAD_DIM)
    CMAX, RUN_MAX, RUN_SUM, FLAGS = 0, 1, 2, 3

    @pl.when(t == 0)
    def _():
        for sq in range(SEQS_PER_STEP):
            for n in range(nb):
                kb = k_ref[sq, n * MOBA_BLOCK:(n + 1) * MOBA_BLOCK, :].astype(_F32)
                kmean_ref[sq, n:n + 1, :] = (
                    jnp.sum(kb, axis=0, keepdims=True) * (1.0 / MOBA_BLOCK))
        s_ref[...] = jnp.full(s_ref.shape, MASKED, _F32)
        acc_ref[...] = jnp.zeros(acc_ref.shape, _F32)
        st_ref[CMAX] = jnp.full(st_ref.shape[1:], MASKED, _F32)
        st_ref[RUN_MAX] = jnp.full(st_ref.shape[1:], MASKED, _F32)
        st_ref[RUN_SUM] = jnp.ones(st_ref.shape[1:], _F32)
        for r in range(BLOCKS_PER_CHUNK):
            st_ref[FLAGS + r] = jnp.zeros(st_ref.shape[1:], _F32)

    j = jnp.minimum(t, nb - 1)
    prev = jnp.maximum(t - 1, 0)
    q_heads = [q_ref[sq, 0, head(a), :] for sq, a in lanes]

    kmean = [kmean_ref[sq].astype(_BF16) for sq in range(SEQS_PER_STEP)]
    blk = lax.broadcasted_iota(jnp.int32, (nb, tq), 0)
    adj_flag = []
    for c, (sq, a) in enumerate(lanes):
        sel = _select_blocks(_dot(kmean[sq][:, head(a)], q_heads[c]), j)
        is_adj = blk == j - 1
        fsel_ref[c] = jnp.where(is_adj, 0.0, sel)
        flag = jnp.max(jnp.where(is_adj, sel, 0.0), axis=0, keepdims=True)
        adj_flag.append(jnp.where(j == 0, 1.0, flag))

    first = jnp.maximum(j - 1, 0)
    variant = jnp.where(j == 0, 1, 0)
    n_far_chunks = jnp.where(t < nb, lax.shift_right_logical(j, 1), 0)
    prev_far_chunks = lax.shift_right_logical(prev, 1)
    prev_last_block = jnp.where(prev_far_chunks == 0, jnp.maximum(prev - 1, 0),
                                (prev_far_chunks - 1) * BLOCKS_PER_CHUNK)

    def key_chunk(first_row):
        rows = pl.ds(pl.multiple_of(first_row, MOBA_BLOCK), KEY_CHUNK)
        return [k_ref[sq, rows, :] for sq in range(SEQS_PER_STEP)]

    ones = jnp.ones((SUM_ROWS, MOBA_BLOCK), _BF16)

    def block_rows(r):
        return slice(r * MOBA_BLOCK, (r + 1) * MOBA_BLOCK)

    def chunk_max(s, flags):
        cmax = None
        for r in range(BLOCKS_PER_CHUNK):
            bmax = jnp.max(s[block_rows(r)], axis=0, keepdims=True)
            bmax = jnp.where(flags[r] > 0.0, bmax, MASKED)
            cmax = bmax if cmax is None else jnp.maximum(cmax, bmax)
        return cmax

    def set_flags(c, flags):
        for r in range(BLOCKS_PER_CHUNK):
            st_ref[FLAGS + r, c] = flags[r]
        return [st_ref[FLAGS + r, c] for r in range(BLOCKS_PER_CHUNK)]

    def fold_lane(c, first_block, cmax, m, l):
        sq, a = lanes[c]
        m_new = jnp.maximum(m, cmax)
        alpha = jnp.exp2(m - m_new)
        pv = None
        for r in range(BLOCKS_PER_CHUNK):
            offset = jnp.where(st_ref[FLAGS + r, c] > 0.0, -m_new, MASKED)
            p = jnp.exp2(s_ref[c, block_rows(r), :] + offset).astype(_BF16)
            v_t = jnp.concatenate([v_ref[sq, first_block + r, head(a), :], ones], axis=0)
            d = _dot(v_t, p)
            pv = d if pv is None else pv + d
        acc_ref[c] = alpha * acc_ref[c] + pv[:HEAD_DIM]
        return m_new, alpha * l + pv[HEAD_DIM:HEAD_DIM + 1]


    kb = key_chunk(first * MOBA_BLOCK)
    own_score = lambda c: _dot(kb[lanes[c][0]][:, head(lanes[c][1])], q_heads[c])
    score = own_score(0)
    cmax0, outs = [], []
    all_queries = jnp.ones((1, tq), _F32)
    for c, (sq, a) in enumerate(lanes):
        next_score = own_score(c + 1) if c + 1 < n_lanes else None
        _, l_c = fold_lane(c, prev_last_block, st_ref[CMAX, c], st_ref[RUN_MAX, c],
                           st_ref[RUN_SUM, c])
        outs.append(acc_ref[c] / l_c)
        acc_ref[c] = jnp.zeros((HEAD_DIM, tq), _F32)
        s = score + tab_ref[a, variant]
        s_ref[c] = s
        flags = set_flags(c, [adj_flag[c]] + [all_queries] * (BLOCKS_PER_CHUNK - 1))
        cmax0.append(chunk_max(s, flags))
        score = next_score
    for sq in range(SEQS_PER_STEP):
        out_t = jnp.concatenate(outs[sq * N_HEADS:(sq + 1) * N_HEADS], axis=0)
        o_ref[sq] = out_t.T.astype(_BF16)

    def step(i, carry):
        cmax, m, l = carry
        kb = key_chunk(i * KEY_CHUNK)
        score_lane = lambda c: _dot(kb[lanes[c][0]][:, head(lanes[c][1])], q_heads[c])
        first_block = jnp.where(i == 0, first, (i - 1) * BLOCKS_PER_CHUNK)
        new_cmax, new_m, new_l = [], [], []
        score = score_lane(0)
        for c in range(n_lanes):
            next_score = score_lane(c + 1) if c + 1 < n_lanes else None
            m_c, l_c = fold_lane(c, first_block, cmax[c], m[c], l[c])
            s_ref[c] = score
            flags = set_flags(c, [fsel_ref[c, pl.ds(i * BLOCKS_PER_CHUNK + r, 1), :]
                                  for r in range(BLOCKS_PER_CHUNK)])
            new_cmax.append(chunk_max(score, flags))
            new_m.append(m_c)
            new_l.append(l_c)
            score = next_score
        return tuple(new_cmax), tuple(new_m), tuple(new_l)

    init_m = tuple(jnp.full((1, tq), MASKED, _F32) for _ in range(n_lanes))
    init_l = tuple(jnp.zeros((1, tq), _F32) for _ in range(n_lanes))
    cmax, m, l = lax.fori_loop(0, n_far_chunks, step, (tuple(cmax0), init_m, init_l))
    for c in range(n_lanes):
        st_ref[CMAX, c] = cmax[c]
        st_ref[RUN_MAX, c] = m[c]
        st_ref[RUN_SUM, c] = l[c]


def _moba(qv3, k, tables, *, batch, seq_len):
    nb = seq_len // MOBA_BLOCK
    sq = SEQS_PER_STEP
    n_lanes = sq * N_HEADS
    assert nb % BLOCKS_PER_CHUNK == 0 and batch % sq == 0
    qv4 = qv3.reshape(batch, nb, 2 * D_ATTN, MOBA_BLOCK)
    k3 = k.reshape(batch, seq_len, D_ATTN)
    resident = dict(pipeline_mode=pl.Buffered(1))
    out = pl.pallas_call(
        _moba_kernel,
        grid=(batch // sq, nb + 1),
        in_specs=[
            pl.BlockSpec((sq, 1, D_ATTN, MOBA_BLOCK),
                         lambda g, t: (g, jnp.minimum(t, nb - 1), 0, 0)),
            pl.BlockSpec((sq, seq_len, D_ATTN), lambda g, t: (g, 0, 0), **resident),
            pl.BlockSpec((sq, nb, D_ATTN, MOBA_BLOCK), lambda g, t: (g, 0, 1, 0), **resident),
            pl.BlockSpec((N_HEADS, 2, KEY_CHUNK, MOBA_BLOCK), lambda g, t: (0, 0, 0, 0),
                         **resident),
        ],
        out_specs=pl.BlockSpec((sq, MOBA_BLOCK, D_ATTN),
                               lambda g, t: (g, jnp.maximum(t - 1, 0), 0)),
        out_shape=jax.ShapeDtypeStruct((batch, seq_len, D_ATTN), _BF16),
        scratch_shapes=[pltpu.VMEM((sq, nb, D_ATTN), _F32),
                        pltpu.VMEM((n_lanes, nb, MOBA_BLOCK), _F32),
                        pltpu.VMEM((n_lanes, KEY_CHUNK, MOBA_BLOCK), _F32),
                        pltpu.VMEM((n_lanes, HEAD_DIM, MOBA_BLOCK), _F32),
                        pltpu.VMEM((3 + BLOCKS_PER_CHUNK, n_lanes, 1, MOBA_BLOCK), _F32)],
        compiler_params=pltpu.CompilerParams(
            dimension_semantics=("arbitrary", "arbitrary"),
            vmem_limit_bytes=VMEM_LIMIT),
        name="moba_attention",
    )(qv4, k3, qv4, tables)
    return out.reshape(batch * seq_len, D_ATTN)


def _shift_rows(g, prev_rows, lag):
    rolled = pltpu.roll(g, lag, 0)
    rid = lax.broadcasted_iota(jnp.int32, prev_rows.shape, 0)
    head = jnp.where(rid < lag, pltpu.roll(prev_rows, lag, 0), rolled[:CONV_HALO])
    return jnp.concatenate([head, rolled[CONV_HALO:]], axis=0)


def _ffn_kernel(x_ref, p_ref, ypool_ref, yattn_ref, wo_ref, g_post_ref,
                g_pre_ref, wup_ref, convw_ref, convb_ref, wdown_ref, g_ffn_post_ref,
                wple_ref, wplegate_ref, o_ref, carry_ref, act_ref, *, tiles_per_seq):
    i = pl.program_id(0)
    tm = x_ref.shape[0]

    @pl.when(i % tiles_per_seq == 0)
    def _():
        carry_ref[...] = jnp.zeros(carry_ref.shape, _F32)

    y = _dot(ypool_ref[...], wo_ref[0:D_POOL, :]) + _dot(yattn_ref[...], wo_ref[D_POOL:, :])
    x1 = x_ref[...] + _rms(y, g_post_ref[...])
    h = _rms(x1, g_pre_ref[...]).astype(_BF16)

    for c in range(N_FF_CHUNKS):
        cols = slice(c * FF_CHUNK, (c + 1) * FF_CHUNK)
        gate = _dot(h, wup_ref[:, cols])
        val = _dot(h, wup_ref[:, D_FF + c * FF_CHUNK:D_FF + (c + 1) * FF_CHUNK])
        prev = carry_ref[:, cols]
        conv = convb_ref[:, cols] + _shift_rows(gate, prev, 2) * convw_ref[0:1, cols]
        conv = conv + _shift_rows(gate, prev, 1) * convw_ref[1:2, cols]
        conv = conv + gate * convw_ref[2:3, cols]
        carry_ref[:, cols] = gate[tm - CONV_HALO:, :]
        act_ref[:, cols] = (jax.nn.gelu(conv, approximate=True) * val).astype(_BF16)

    x2 = x1 + _rms(_dot(act_ref[...], wdown_ref[...]), g_ffn_post_ref[...])
    ple = _dot(p_ref[...].astype(_BF16), wple_ref[...])
    gate_logit = _dot(x2.astype(_BF16), wplegate_ref[...])
    o_ref[...] = x2 + ple * (1.0 / (1.0 + jnp.exp(-gate_logit)))


def _ffn(x2d, p3d, ypool, yattn, wo, g_post, g_pre, wup, convw, convb,
         wdown, g_ffn_post, wple, wplegate, *, layer, seq_len):
    t = x2d.shape[0]
    tm = ROW_TILE
    rows = lambda width: pl.BlockSpec((tm, width), lambda i: (i, 0))
    const = functools.partial(_layer_block, layer, pipeline_mode=pl.Buffered(1))
    return pl.pallas_call(
        functools.partial(_ffn_kernel, tiles_per_seq=seq_len // tm),
        grid=(t // tm,),
        in_specs=[
            rows(D_MODEL), pl.BlockSpec((None, tm, PLE_DIM), lambda i: (layer, i, 0)),
            rows(D_POOL), rows(D_ATTN),
            const((D_MODEL, D_MODEL)), const((1, D_MODEL)),
            const((1, D_MODEL)),
            const((D_MODEL, 2 * D_FF)),
            const((CONV_WIDTH, D_FF)), const((1, D_FF)),
            const((D_FF, D_MODEL)), const((1, D_MODEL)),
            const((PLE_DIM, D_MODEL)), const((D_MODEL, D_MODEL)),
        ],
        out_specs=rows(D_MODEL),
        out_shape=jax.ShapeDtypeStruct((t, D_MODEL), _F32),
        scratch_shapes=[pltpu.VMEM((CONV_HALO, D_FF), _F32), pltpu.VMEM((tm, D_FF), _BF16)],
        compiler_params=pltpu.CompilerParams(
            dimension_semantics=("arbitrary",), vmem_limit_bytes=VMEM_LIMIT),
        name="ffn_block",
    )(x2d, p3d, ypool, yattn, wo, g_post, g_pre, wup, convw, convb,
      wdown, g_ffn_post, wple, wplegate)


def kernel(x, p, rel_bias, g_mix_pre, g_mix_post, g_ffn_pre, g_ffn_post, w_in, w_pool,
           pool_scale, w_out, w_up, conv_w, conv_b, w_down, w_ple, w_ple_gate):
    batch, seq_len, d_model = x.shape
    depth = w_in.shape[0]
    assert d_model == D_MODEL and seq_len % ROW_TILE == 0 and ROW_TILE % MOBA_BLOCK == 0
    t = batch * seq_len
    rows = lambda v: v.reshape(depth, 1, -1).astype(_F32)
    bf16 = lambda w: w.astype(_BF16)
    w_qv_t = jnp.concatenate(
        [w_in[..., D_POOL:D_POOL + D_ATTN], w_in[..., D_POOL + 2 * D_ATTN:]], axis=-1,
    ).astype(_BF16).transpose(0, 2, 1)
    w_in_b, w_pool_b, w_out_b, w_up_b, w_down_b = map(bf16, (w_in, w_pool, w_out, w_up, w_down))
    w_ple_b, w_ple_gate_b = bf16(w_ple), bf16(w_ple_gate)
    p3d = p.reshape(depth, t, PLE_DIM).astype(_F32)

    tables = _bias_tables(rel_bias)
    xf = x.reshape(t, d_model).astype(_F32)
    for i in range(depth):
        k, qv3, ypool = _proj_pool(
            xf, rows(g_mix_pre), w_in_b, w_qv_t, w_pool_b, rows(pool_scale),
            layer=i, seq_len=seq_len)
        yattn = _moba(qv3, k, tables, batch=batch, seq_len=seq_len)
        xf = _ffn(
            xf, p3d, ypool, yattn, w_out_b, rows(g_mix_post), rows(g_ffn_pre), w_up_b,
            conv_w.astype(_F32), rows(conv_b), w_down_b, rows(g_ffn_post), w_ple_b,
            w_ple_gate_b, layer=i, seq_len=seq_len)
    return xf.reshape(batch, seq_len, d_model).astype(x.dtype)
```

```python
import functools
import math

import numpy as np
import jax
import jax.numpy as jnp
from jax import lax
from jax.experimental import pallas as pl
from jax.experimental.pallas import tpu as pltpu

D_MODEL = 1024
PLE_DIM = 256
D_POOL = 512
POOL_WINDOWS = (2, 4, 8, 16)
POOL_GROUP = D_POOL // len(POOL_WINDOWS)
D_ATTN = D_MODEL - D_POOL
N_HEADS = 8
HEAD_DIM = D_ATTN // N_HEADS
MOBA_BLOCK = 256
MOBA_TOPK = 3
NUM_BUCKETS = 32
MAX_DISTANCE = 128
D_FF = 2816
CONV_WIDTH = 3
EPS = 1e-6

SEQS_PER_STEP = 2
POOL_HIST = 16
POOL_HALO = 2 * POOL_HIST
CONV_HALO = 8
FF_CHUNK = 256
N_FF_CHUNKS = D_FF // FF_CHUNK
ROW_TILE = 512
MASKED = -1e30
LOG2_E = math.log2(math.e)
BLOCKS_PER_CHUNK = 2
KEY_CHUNK = BLOCKS_PER_CHUNK * MOBA_BLOCK
SUM_ROWS = 16
VMEM_LIMIT = 56 * 1024 * 1024

_BF16 = jnp.bfloat16
_F32 = jnp.float32


def _rms(x, g):
    return x * lax.rsqrt(jnp.mean(x * x, axis=-1, keepdims=True) + EPS) * g


def _dot(a, b):
    return jnp.dot(a, b, preferred_element_type=_F32)


def _bucket_tables():
    key = np.arange(MOBA_BLOCK)[:, None]
    qry = np.arange(MOBA_BLOCK)[None, :]

    def bucket(dist):
        n = np.maximum(dist, 0)
        max_exact = NUM_BUCKETS // 2
        nf = np.maximum(n, 1).astype(np.float32)
        large = max_exact + (np.log(nf / max_exact) / math.log(MAX_DISTANCE / max_exact)
                             * (NUM_BUCKETS - max_exact)).astype(np.int32)
        large = np.minimum(large, NUM_BUCKETS - 1)
        return np.where(n < max_exact, n, large).astype(np.int32)

    d_own = qry - key
    own = np.where(d_own >= 0, bucket(d_own), -1).astype(np.int32)
    adj = bucket(d_own + MOBA_BLOCK)
    return own, adj


def _bias_table_kernel(rb_ref, own_idx_ref, adj_idx_ref, tab_ref):
    h = pl.program_id(0)
    far = rb_ref[NUM_BUCKETS - 1, h]
    own_idx = own_idx_ref[...]
    adj_idx = adj_idx_ref[...]
    own = jnp.zeros(own_idx.shape, _F32)
    adj = jnp.zeros(adj_idx.shape, _F32)
    for b in range(NUM_BUCKETS):
        val = (rb_ref[b, h] - far) * LOG2_E
        own = jnp.where(own_idx == b, val, own)
        adj = jnp.where(adj_idx == b, val, adj)
    own = jnp.where(own_idx < 0, MASKED, own)
    tab_ref[0, 0, 0:MOBA_BLOCK, :] = adj
    tab_ref[0, 0, MOBA_BLOCK:, :] = own
    tab_ref[0, 1, 0:MOBA_BLOCK, :] = own
    tab_ref[0, 1, MOBA_BLOCK:, :] = jnp.full(own.shape, MASKED, _F32)


def _bias_tables(rel_bias):
    own_idx, adj_idx = _bucket_tables()
    blk = pl.BlockSpec((MOBA_BLOCK, MOBA_BLOCK), lambda h: (0, 0))
    return pl.pallas_call(
        _bias_table_kernel,
        grid=(N_HEADS,),
        in_specs=[pl.BlockSpec(memory_space=pltpu.SMEM), blk, blk],
        out_specs=pl.BlockSpec((1, 2, KEY_CHUNK, MOBA_BLOCK), lambda h: (h, 0, 0, 0)),
        out_shape=jax.ShapeDtypeStruct((N_HEADS, 2, KEY_CHUNK, MOBA_BLOCK), _F32),
        name="bias_tables",
    )(rel_bias.astype(_F32), jnp.asarray(own_idx), jnp.asarray(adj_idx))


def _proj_pool_kernel(x_ref, g_ref, win_ref, wqv_ref, wpool_ref, pscale_ref,
                      k_ref, qv_ref, ypool_ref, ext_ref, lvl_ref, *, tiles_per_seq):
    i = pl.program_id(0)
    tm = x_ref.shape[0]
    h = _rms(x_ref[...], g_ref[...]).astype(_BF16)
    seq_tile = i % tiles_per_seq

    @pl.when(seq_tile == 0)
    def _():
        ext_ref[0:POOL_HALO, :] = jnp.zeros((POOL_HALO, D_POOL), _F32)

    u = _dot(h, win_ref[:, 0:D_POOL])
    ext_ref[POOL_HALO:POOL_HALO + tm, :] = u

    k_ref[...] = _dot(h, win_ref[:, D_POOL + D_ATTN:D_POOL + 2 * D_ATTN]).astype(_BF16)

    qv = lax.dot_general(wqv_ref[...], h, (((1,), (1,)), ((), ())),
                         preferred_element_type=_F32)
    row = lax.broadcasted_iota(jnp.int32, qv.shape, 0)
    qv = jnp.where(row < D_ATTN, qv * (HEAD_DIM ** -0.5 * LOG2_E), qv).astype(_BF16)
    for c in range(tm // MOBA_BLOCK):
        qv_ref[c] = qv[:, c * MOBA_BLOCK:(c + 1) * MOBA_BLOCK]

    end = POOL_HALO + tm
    pos = seq_tile * tm + lax.broadcasted_iota(jnp.int32, (tm, 1), 0)
    for g, w in enumerate(POOL_WINDOWS):
        cols = slice(g * POOL_GROUP, (g + 1) * POOL_GROUP)
        levels = w.bit_length() - 1
        assert w == 1 << levels and 1 <= levels <= 4
        read = lambda lo, hi: ext_ref[lo:hi, cols]
        for lvl in range(1, levels + 1):
            delay = 1 << (lvl - 1)
            lo = POOL_HALO if lvl == levels else 8 * lvl
            acc = read(lo, end) + read(lo - delay, end - delay)
            if lvl < levels:
                slot = lvl % 2
                lvl_ref[slot, lo:end, :] = acc
                read = lambda lo, hi, slot=slot: lvl_ref[slot, lo:hi, :]
        u_g = u[:, cols]
        count = jnp.minimum(pos + 1, w).astype(_F32)
        pooled = (acc / count - u_g).astype(_BF16)
        mixed = _dot(pooled, wpool_ref[g]) * pscale_ref[:, cols]
        ypool_ref[:, cols] = mixed.astype(_BF16)
    ext_ref[POOL_HALO - POOL_HIST:POOL_HALO, :] = u[tm - POOL_HIST:, :]


def _layer_block(layer, shape, **kwargs):
    return pl.BlockSpec((None,) + shape, lambda i: (layer,) + (0,) * len(shape), **kwargs)


def _proj_pool(x2d, g, win, wqv, wpool, pscale, *, layer, seq_len):
    t = x2d.shape[0]
    tm = ROW_TILE
    const = functools.partial(_layer_block, layer)
    return pl.pallas_call(
        functools.partial(_proj_pool_kernel, tiles_per_seq=seq_len // tm),
        grid=(t // tm,),
        in_specs=[
            pl.BlockSpec((tm, D_MODEL), lambda i: (i, 0)),
            const((1, D_MODEL)),
            const((D_MODEL, D_POOL + 3 * D_ATTN)),
            const((2 * D_ATTN, D_MODEL)),
            const((len(POOL_WINDOWS), POOL_GROUP, POOL_GROUP)),
            const((1, D_POOL)),
        ],
        out_specs=[
            pl.BlockSpec((tm, D_ATTN), lambda i: (i, 0)),
            pl.BlockSpec((tm // MOBA_BLOCK, 2 * D_ATTN, MOBA_BLOCK), lambda i: (i, 0, 0)),
            pl.BlockSpec((tm, D_POOL), lambda i: (i, 0)),
        ],
        out_shape=[
            jax.ShapeDtypeStruct((t, D_ATTN), _BF16),
            jax.ShapeDtypeStruct((t // MOBA_BLOCK, 2 * D_ATTN, MOBA_BLOCK), _BF16),
            jax.ShapeDtypeStruct((t, D_POOL), _BF16),
        ],
        scratch_shapes=[pltpu.VMEM((POOL_HALO + tm, D_POOL), _F32),
                        pltpu.VMEM((2, POOL_HALO + tm, POOL_GROUP), _F32)],
        compiler_params=pltpu.CompilerParams(
            dimension_semantics=("arbitrary",), vmem_limit_bytes=VMEM_LIMIT),
        name="proj_pool",
    )(x2d, g, win, wqv, wpool, pscale)


def _select_blocks(gate, n_past):
    blk = lax.broadcasted_iota(jnp.int32, gate.shape, 0)
    n_blocks = gate.shape[0]
    g = jnp.where(blk < n_past, gate, -jnp.inf)
    sel = jnp.zeros(gate.shape, _F32)
    for _ in range(MOBA_TOPK):
        mx = jnp.max(g, axis=0, keepdims=True)
        is_max = jnp.logical_and(g == mx, mx > -jnp.inf)
        first = jnp.min(jnp.where(is_max, blk, n_blocks), axis=0, keepdims=True)
        pick = blk == first
        sel = jnp.where(pick, 1.0, sel)
        g = jnp.where(pick, -jnp.inf, g)
    return sel


def _moba_kernel(q_ref, k_ref, v_ref, tab_ref, o_ref, kmean_ref, fsel_ref, s_ref, acc_ref,
                 st_ref):
    t = pl.program_id(1)
    nb = v_ref.shape[1]
    tq = q_ref.shape[3]
    lanes = [(sq, a) for sq in range(SEQS_PER_STEP) for a in range(N_HEADS)]
    n_lanes = len(lanes)
    head = lambda a: slice(a * HEAD_DIM, (a + 1) * HEAD_DIM)
    CMAX, RUN_MAX, RUN_SUM, FLAGS = 0, 1, 2, 3

    @pl.when(t == 0)
    def _():
        for sq in range(SEQS_PER_STEP):
            for n in range(nb):
                kb = k_ref[sq, n * MOBA_BLOCK:(n + 1) * MOBA_BLOCK, :].astype(_F32)
                kmean_ref[sq, n:n + 1, :] = (
                    jnp.sum(kb, axis=0, keepdims=True) * (1.0 / MOBA_BLOCK))
        s_ref[...] = jnp.full(s_ref.shape, MASKED, _F32)
        acc_ref[...] = jnp.zeros(acc_ref.shape, _F32)
        st_ref[CMAX] = jnp.full(st_ref.shape[1:], MASKED, _F32)
        st_ref[RUN_MAX] = jnp.full(st_ref.shape[1:], MASKED, _F32)
        st_ref[RUN_SUM] = jnp.ones(st_ref.shape[1:], _F32)
        for r in range(BLOCKS_PER_CHUNK):
            st_ref[FLAGS + r] = jnp.zeros(st_ref.shape[1:], _F32)

    j = jnp.minimum(t, nb - 1)
    prev = jnp.maximum(t - 1, 0)
    q_heads = [q_ref[sq, 0, head(a), :] for sq, a in lanes]

    kmean = [kmean_ref[sq].astype(_BF16) for sq in range(SEQS_PER_STEP)]
    blk = lax.broadcasted_iota(jnp.int32, (nb, tq), 0)
    adj_flag = []
    for c, (sq, a) in enumerate(lanes):
        sel = _select_blocks(_dot(kmean[sq][:, head(a)], q_heads[c]), j)
        is_adj = blk == j - 1
        fsel_ref[c] = jnp.where(is_adj, 0.0, sel)
        flag = jnp.max(jnp.where(is_adj, sel, 0.0), axis=0, keepdims=True)
        adj_flag.append(jnp.where(j == 0, 1.0, flag))

    first = jnp.maximum(j - 1, 0)
    variant = jnp.where(j == 0, 1, 0)
    n_far_chunks = jnp.where(t < nb, lax.shift_right_logical(j, 1), 0)
    prev_far_chunks = lax.shift_right_logical(prev, 1)
    prev_last_block = jnp.where(prev_far_chunks == 0, jnp.maximum(prev - 1, 0),
                                (prev_far_chunks - 1) * BLOCKS_PER_CHUNK)

    def key_chunk(first_row):
        rows = pl.ds(pl.multiple_of(first_row, MOBA_BLOCK), KEY_CHUNK)
        return [k_ref[sq, rows, :] for sq in range(SEQS_PER_STEP)]

    ones = jnp.ones((SUM_ROWS, MOBA_BLOCK), _BF16)

    def block_rows(r):
        return slice(r * MOBA_BLOCK, (r + 1) * MOBA_BLOCK)

    def chunk_max(s, flags):
        cmax = None
        for r in range(BLOCKS_PER_CHUNK):
            bmax = jnp.max(s[block_rows(r)], axis=0, keepdims=True)
            bmax = jnp.where(flags[r] > 0.0, bmax, MASKED)
            cmax = bmax if cmax is None else jnp.maximum(cmax, bmax)
        return cmax

    def set_flags(c, flags):
        for r in range(BLOCKS_PER_CHUNK):
            st_ref[FLAGS + r, c] = flags[r]
        return [st_ref[FLAGS + r, c] for r in range(BLOCKS_PER_CHUNK)]

    def fold_lane(c, first_block, cmax, m, l):
        sq, a = lanes[c]
        m_new = jnp.maximum(m, cmax)
        alpha = jnp.exp2(m - m_new)
        pv = None
        for r in range(BLOCKS_PER_CHUNK):
            offset = jnp.where(st_ref[FLAGS + r, c] > 0.0, -m_new, MASKED)
            p = jnp.exp2(s_ref[c, block_rows(r), :] + offset).astype(_BF16)
            v_t = jnp.concatenate([v_ref[sq, first_block + r, head(a), :], ones], axis=0)
            d = _dot(v_t, p)
            pv = d if pv is None else pv + d
        acc_ref[c] = alpha * acc_ref[c] + pv[:HEAD_DIM]
        return m_new, alpha * l + pv[HEAD_DIM:HEAD_DIM + 1]


    kb = key_chunk(first * MOBA_BLOCK)
    own_score = lambda c: _dot(kb[lanes[c][0]][:, head(lanes[c][1])], q_heads[c])
    score = own_score(0)
    cmax0, outs = [], []
    all_queries = jnp.ones((1, tq), _F32)
    for c, (sq, a) in enumerate(lanes):
        next_score = own_score(c + 1) if c + 1 < n_lanes else None
        _, l_c = fold_lane(c, prev_last_block, st_ref[CMAX, c], st_ref[RUN_MAX, c],
                           st_ref[RUN_SUM, c])
        outs.append(acc_ref[c] / l_c)
        acc_ref[c] = jnp.zeros((HEAD_DIM, tq), _F32)
        s = score + tab_ref[a, variant]
        s_ref[c] = s
        flags = set_flags(c, [adj_flag[c]] + [all_queries] * (BLOCKS_PER_CHUNK - 1))
        cmax0.append(chunk_max(s, flags))
        score = next_score
    for sq in range(SEQS_PER_STEP):
        out_t = jnp.concatenate(outs[sq * N_HEADS:(sq + 1) * N_HEADS], axis=0)
        o_ref[sq] = out_t.T.astype(_BF16)

    def step(i, carry):
        cmax, m, l = carry
        kb = key_chunk(i * KEY_CHUNK)
        score_lane = lambda c: _dot(kb[lanes[c][0]][:, head(lanes[c][1])], q_heads[c])
        first_block = jnp.where(i == 0, first, (i - 1) * BLOCKS_PER_CHUNK)
        new_cmax, new_m, new_l = [], [], []
        score = score_lane(0)
        for c in range(n_lanes):
            next_score = score_lane(c + 1) if c + 1 < n_lanes else None
            m_c, l_c = fold_lane(c, first_block, cmax[c], m[c], l[c])
            s_ref[c] = score
            flags = set_flags(c, [fsel_ref[c, pl.ds(i * BLOCKS_PER_CHUNK + r, 1), :]
                                  for r in range(BLOCKS_PER_CHUNK)])
            new_cmax.append(chunk_max(score, flags))
            new_m.append(m_c)
            new_l.append(l_c)
            score = next_score
        return tuple(new_cmax), tuple(new_m), tuple(new_l)

    init_m = tuple(jnp.full((1, tq), MASKED, _F32) for _ in range(n_lanes))
    init_l = tuple(jnp.zeros((1, tq), _F32) for _ in range(n_lanes))
    cmax, m, l = lax.fori_loop(0, n_far_chunks, step, (tuple(cmax0), init_m, init_l))
    for c in range(n_lanes):
        st_ref[CMAX, c] = cmax[c]
        st_ref[RUN_MAX, c] = m[c]
        st_ref[RUN_SUM, c] = l[c]


def _moba(qv3, k, tables, *, batch, seq_len):
    nb = seq_len // MOBA_BLOCK
    sq = SEQS_PER_STEP
    n_lanes = sq * N_HEADS
    assert nb % BLOCKS_PER_CHUNK == 0 and batch % sq == 0
    qv4 = qv3.reshape(batch, nb, 2 * D_ATTN, MOBA_BLOCK)
    k3 = k.reshape(batch, seq_len, D_ATTN)
    resident = dict(pipeline_mode=pl.Buffered(1))
    out = pl.pallas_call(
        _moba_kernel,
        grid=(batch // sq, nb + 1),
        in_specs=[
            pl.BlockSpec((sq, 1, D_ATTN, MOBA_BLOCK),
                         lambda g, t: (g, jnp.minimum(t, nb - 1), 0, 0)),
            pl.BlockSpec((sq, seq_len, D_ATTN), lambda g, t: (g, 0, 0), **resident),
            pl.BlockSpec((sq, nb, D_ATTN, MOBA_BLOCK), lambda g, t: (g, 0, 1, 0), **resident),
            pl.BlockSpec((N_HEADS, 2, KEY_CHUNK, MOBA_BLOCK), lambda g, t: (0, 0, 0, 0),
                         **resident),
        ],
        out_specs=pl.BlockSpec((sq, MOBA_BLOCK, D_ATTN),
                               lambda g, t: (g, jnp.maximum(t - 1, 0), 0)),
        out_shape=jax.ShapeDtypeStruct((batch, seq_len, D_ATTN), _BF16),
        scratch_shapes=[pltpu.VMEM((sq, nb, D_ATTN), _F32),
                        pltpu.VMEM((n_lanes, nb, MOBA_BLOCK), _F32),
                        pltpu.VMEM((n_lanes, KEY_CHUNK, MOBA_BLOCK), _F32),
                        pltpu.VMEM((n_lanes, HEAD_DIM, MOBA_BLOCK), _F32),
                        pltpu.VMEM((3 + BLOCKS_PER_CHUNK, n_lanes, 1, MOBA_BLOCK), _F32)],
        compiler_params=pltpu.CompilerParams(
            dimension_semantics=("arbitrary", "arbitrary"),
            vmem_limit_bytes=VMEM_LIMIT),
        name="moba_attention",
    )(qv4, k3, qv4, tables)
    return out.reshape(batch * seq_len, D_ATTN)


def _shift_rows(g, prev_rows, lag):
    rolled = pltpu.roll(g, lag, 0)
    rid = lax.broadcasted_iota(jnp.int32, prev_rows.shape, 0)
    head = jnp.where(rid < lag, pltpu.roll(prev_rows, lag, 0), rolled[:CONV_HALO])
    return jnp.concatenate([head, rolled[CONV_HALO:]], axis=0)


def _ffn_kernel(x_ref, p_ref, ypool_ref, yattn_ref, wo_ref, g_post_ref,
                g_pre_ref, wup_ref, convw_ref, convb_ref, wdown_ref, g_ffn_post_ref,
                wple_ref, wplegate_ref, o_ref, carry_ref, act_ref, *, tiles_per_seq):
    i = pl.program_id(0)
    tm = x_ref.shape[0]

    @pl.when(i % tiles_per_seq == 0)
    def _():
        carry_ref[...] = jnp.zeros(carry_ref.shape, _F32)

    y = _dot(ypool_ref[...], wo_ref[0:D_POOL, :]) + _dot(yattn_ref[...], wo_ref[D_POOL:, :])
    ple = _dot(p_ref[...].astype(_BF16), wple_ref[...])
    x1 = x_ref[...] + _rms(y, g_post_ref[...])
    h = _rms(x1, g_pre_ref[...]).astype(_BF16)

    for c in range(N_FF_CHUNKS):
        cols = slice(c * FF_CHUNK, (c + 1) * FF_CHUNK)
        gate = _dot(h, wup_ref[:, cols])
        val = _dot(h, wup_ref[:, D_FF + c * FF_CHUNK:D_FF + (c + 1) * FF_CHUNK])
        prev = carry_ref[:, cols]
        conv = convb_ref[:, cols] + _shift_rows(gate, prev, 2) * convw_ref[0:1, cols]
        conv = conv + _shift_rows(gate, prev, 1) * convw_ref[1:2, cols]
        conv = conv + gate * convw_ref[2:3, cols]
        carry_ref[:, cols] = gate[tm - CONV_HALO:, :]
        act_ref[:, cols] = (jax.nn.gelu(conv, approximate=True) * val).astype(_BF16)

    x2 = x1 + _rms(_dot(act_ref[...], wdown_ref[...]), g_ffn_post_ref[...])
    gate_logit = _dot(x2.astype(_BF16), wplegate_ref[...])
    o_ref[...] = x2 + ple * (1.0 / (1.0 + jnp.exp(-gate_logit)))


def _ffn(x2d, p3d, ypool, yattn, wo, g_post, g_pre, wup, convw, convb,
         wdown, g_ffn_post, wple, wplegate, *, layer, seq_len):
    t = x2d.shape[0]
    tm = ROW_TILE
    rows = lambda width: pl.BlockSpec((tm, width), lambda i: (i, 0))
    const = functools.partial(_layer_block, layer, pipeline_mode=pl.Buffered(1))
    return pl.pallas_call(
        functools.partial(_ffn_kernel, tiles_per_seq=seq_len // tm),
        grid=(t // tm,),
        in_specs=[
            rows(D_MODEL), pl.BlockSpec((None, tm, PLE_DIM), lambda i: (layer, i, 0)),
            rows(D_POOL), rows(D_ATTN),
            const((D_MODEL, D_MODEL)), const((1, D_MODEL)),
            const((1, D_MODEL)),
            const((D_MODEL, 2 * D_FF)),
            const((CONV_WIDTH, D_FF)), const((1, D_FF)),
            const((D_FF, D_MODEL)), const((1, D_MODEL)),
            const((PLE_DIM, D_MODEL)), const((D_MODEL, D_MODEL)),
        ],
        out_specs=rows(D_MODEL),
        out_shape=jax.ShapeDtypeStruct((t, D_MODEL), _F32),
        scratch_shapes=[pltpu.VMEM((CONV_HALO, D_FF), _F32), pltpu.VMEM((tm, D_FF), _BF16)],
        compiler_params=pltpu.CompilerParams(
            dimension_semantics=("arbitrary",), vmem_limit_bytes=VMEM_LIMIT),
        name="ffn_block",
    )(x2d, p3d, ypool, yattn, wo, g_post, g_pre, wup, convw, convb,
      wdown, g_ffn_post, wple, wplegate)


def kernel(x, p, rel_bias, g_mix_pre, g_mix_post, g_ffn_pre, g_ffn_post, w_in, w_pool,
           pool_scale, w_out, w_up, conv_w, conv_b, w_down, w_ple, w_ple_gate):
    batch, seq_len, d_model = x.shape
    depth = w_in.shape[0]
    assert d_model == D_MODEL and seq_len % ROW_TILE == 0 and ROW_TILE % MOBA_BLOCK == 0
    t = batch * seq_len
    rows = lambda v: v.reshape(depth, 1, -1).astype(_F32)
    bf16 = lambda w: w.astype(_BF16)
    w_qv_t = jnp.concatenate(
        [w_in[..., D_POOL:D_POOL + D_ATTN], w_in[..., D_POOL + 2 * D_ATTN:]], axis=-1,
    ).astype(_BF16).transpose(0, 2, 1)
    w_in_b, w_pool_b, w_out_b, w_up_b, w_down_b = map(bf16, (w_in, w_pool, w_out, w_up, w_down))
    w_ple_b, w_ple_gate_b = bf16(w_ple), bf16(w_ple_gate)
    p3d = p.reshape(depth, t, PLE_DIM).astype(_F32)

    tables = _bias_tables(rel_bias)
    xf = x.reshape(t, d_model).astype(_F32)
    for i in range(depth):
        k, qv3, ypool = _proj_pool(
            xf, rows(g_mix_pre), w_in_b, w_qv_t, w_pool_b, rows(pool_scale),
            layer=i, seq_len=seq_len)
        yattn = _moba(qv3, k, tables, batch=batch, seq_len=seq_len)
        xf = _ffn(
            xf, p3d, ypool, yattn, w_out_b, rows(g_mix_post), rows(g_ffn_pre), w_up_b,
            conv_w.astype(_F32), rows(conv_b), w_down_b, rows(g_ffn_post), w_ple_b,
            w_ple_gate_b, layer=i, seq_len=seq_len)
    return xf.reshape(batch, seq_len, d_model).astype(x.dtype)
```

```python
import functools
import math

import numpy as np
import jax
import jax.numpy as jnp
from jax import lax
from jax.experimental import pallas as pl
from jax.experimental.pallas import tpu as pltpu

D_MODEL = 1024
PLE_DIM = 256
D_POOL = 512
POOL_WINDOWS = (2, 4, 8, 16)
POOL_GROUP = D_POOL // len(POOL_WINDOWS)
D_ATTN = D_MODEL - D_POOL
N_HEADS = 8
HEAD_DIM = D_ATTN // N_HEADS
MOBA_BLOCK = 256
MOBA_TOPK = 3
NUM_BUCKETS = 32
MAX_DISTANCE = 128
D_FF = 2816
CONV_WIDTH = 3
EPS = 1e-6

SEQS_PER_STEP = 2
POOL_HIST = 16
POOL_HALO = 2 * POOL_HIST
CONV_HALO = 8
FF_CHUNK = 256
N_FF_CHUNKS = D_FF // FF_CHUNK
ROW_TILE = 512
PROJ_ROW_TILE = 1024
MASKED = -1e30
LOG2_E = math.log2(math.e)
BLOCKS_PER_CHUNK = 2
KEY_CHUNK = BLOCKS_PER_CHUNK * MOBA_BLOCK
SUM_ROWS = 16
VMEM_LIMIT = 56 * 1024 * 1024

_BF16 = jnp.bfloat16
_F32 = jnp.float32


def _rms(x, g):
    return x * lax.rsqrt(jnp.mean(x * x, axis=-1, keepdims=True) + EPS) * g


def _dot(a, b):
    return jnp.dot(a, b, preferred_element_type=_F32)


def _bucket_tables():
    key = np.arange(MOBA_BLOCK)[:, None]
    qry = np.arange(MOBA_BLOCK)[None, :]

    def bucket(dist):
        n = np.maximum(dist, 0)
        max_exact = NUM_BUCKETS // 2
        nf = np.maximum(n, 1).astype(np.float32)
        large = max_exact + (np.log(nf / max_exact) / math.log(MAX_DISTANCE / max_exact)
                             * (NUM_BUCKETS - max_exact)).astype(np.int32)
        large = np.minimum(large, NUM_BUCKETS - 1)
        return np.where(n < max_exact, n, large).astype(np.int32)

    d_own = qry - key
    own = np.where(d_own >= 0, bucket(d_own), -1).astype(np.int32)
    adj = bucket(d_own + MOBA_BLOCK)
    return own, adj


def _bias_table_kernel(rb_ref, own_idx_ref, adj_idx_ref, tab_ref):
    h = pl.program_id(0)
    far = rb_ref[NUM_BUCKETS - 1, h]
    own_idx = own_idx_ref[...]
    adj_idx = adj_idx_ref[...]
    own = jnp.zeros(own_idx.shape, _F32)
    adj = jnp.zeros(adj_idx.shape, _F32)
    for b in range(NUM_BUCKETS):
        val = (rb_ref[b, h] - far) * LOG2_E
        own = jnp.where(own_idx == b, val, own)
        adj = jnp.where(adj_idx == b, val, adj)
    own = jnp.where(own_idx < 0, MASKED, own)
    tab_ref[0, 0, 0:MOBA_BLOCK, :] = adj
    tab_ref[0, 0, MOBA_BLOCK:, :] = own
    tab_ref[0, 1, 0:MOBA_BLOCK, :] = own
    tab_ref[0, 1, MOBA_BLOCK:, :] = jnp.full(own.shape, MASKED, _F32)


def _bias_tables(rel_bias):
    own_idx, adj_idx = _bucket_tables()
    blk = pl.BlockSpec((MOBA_BLOCK, MOBA_BLOCK), lambda h: (0, 0))
    return pl.pallas_call(
        _bias_table_kernel,
        grid=(N_HEADS,),
        in_specs=[pl.BlockSpec(memory_space=pltpu.SMEM), blk, blk],
        out_specs=pl.BlockSpec((1, 2, KEY_CHUNK, MOBA_BLOCK), lambda h: (h, 0, 0, 0)),
        out_shape=jax.ShapeDtypeStruct((N_HEADS, 2, KEY_CHUNK, MOBA_BLOCK), _F32),
        name="bias_tables",
    )(rel_bias.astype(_F32), jnp.asarray(own_idx), jnp.asarray(adj_idx))


def _proj_pool_kernel(x_ref, g_ref, win_ref, wqv_ref, wpool_ref, pscale_ref,
                      k_ref, qv_ref, ypool_ref, ext_ref, lvl_ref, *, tiles_per_seq):
    i = pl.program_id(0)
    tm = x_ref.shape[0]
    h = _rms(x_ref[...], g_ref[...]).astype(_BF16)
    seq_tile = i % tiles_per_seq

    @pl.when(seq_tile == 0)
    def _():
        ext_ref[0:POOL_HALO, :] = jnp.zeros((POOL_HALO, D_POOL), _F32)

    u = _dot(h, win_ref[:, 0:D_POOL])
    ext_ref[POOL_HALO:POOL_HALO + tm, :] = u

    k_ref[...] = _dot(h, win_ref[:, D_POOL + D_ATTN:D_POOL + 2 * D_ATTN]).astype(_BF16)

    qv = lax.dot_general(wqv_ref[...], h, (((1,), (1,)), ((), ())),
                         preferred_element_type=_F32)
    row = lax.broadcasted_iota(jnp.int32, qv.shape, 0)
    qv = jnp.where(row < D_ATTN, qv * (HEAD_DIM ** -0.5 * LOG2_E), qv).astype(_BF16)
    for c in range(tm // MOBA_BLOCK):
        qv_ref[c] = qv[:, c * MOBA_BLOCK:(c + 1) * MOBA_BLOCK]

    end = POOL_HALO + tm
    pos = seq_tile * tm + lax.broadcasted_iota(jnp.int32, (tm, 1), 0)
    for g, w in enumerate(POOL_WINDOWS):
        cols = slice(g * POOL_GROUP, (g + 1) * POOL_GROUP)
        levels = w.bit_length() - 1
        assert w == 1 << levels and 1 <= levels <= 4
        read = lambda lo, hi: ext_ref[lo:hi, cols]
        for lvl in range(1, levels + 1):
            delay = 1 << (lvl - 1)
            lo = POOL_HALO if lvl == levels else 8 * lvl
            acc = read(lo, end) + read(lo - delay, end - delay)
            if lvl < levels:
                slot = lvl % 2
                lvl_ref[slot, lo:end, :] = acc
                read = lambda lo, hi, slot=slot: lvl_ref[slot, lo:hi, :]
        u_g = u[:, cols]
        count = jnp.minimum(pos + 1, w).astype(_F32)
        pooled = (acc / count - u_g).astype(_BF16)
        mixed = _dot(pooled, wpool_ref[g]) * pscale_ref[:, cols]
        ypool_ref[:, cols] = mixed.astype(_BF16)
    ext_ref[POOL_HALO - POOL_HIST:POOL_HALO, :] = u[tm - POOL_HIST:, :]


def _layer_block(layer, shape, **kwargs):
    return pl.BlockSpec((None,) + shape, lambda i: (layer,) + (0,) * len(shape), **kwargs)


def _proj_pool(x2d, g, win, wqv, wpool, pscale, *, layer, seq_len):
    t = x2d.shape[0]
    tm = PROJ_ROW_TILE
    const = functools.partial(_layer_block, layer)
    return pl.pallas_call(
        functools.partial(_proj_pool_kernel, tiles_per_seq=seq_len // tm),
        grid=(t // tm,),
        in_specs=[
            pl.BlockSpec((tm, D_MODEL), lambda i: (i, 0)),
            const((1, D_MODEL)),
            const((D_MODEL, D_POOL + 3 * D_ATTN)),
            const((2 * D_ATTN, D_MODEL)),
            const((len(POOL_WINDOWS), POOL_GROUP, POOL_GROUP)),
            const((1, D_POOL)),
        ],
        out_specs=[
            pl.BlockSpec((tm, D_ATTN), lambda i: (i, 0)),
            pl.BlockSpec((tm // MOBA_BLOCK, 2 * D_ATTN, MOBA_BLOCK), lambda i: (i, 0, 0)),
            pl.BlockSpec((tm, D_POOL), lambda i: (i, 0)),
        ],
        out_shape=[
            jax.ShapeDtypeStruct((t, D_ATTN), _BF16),
            jax.ShapeDtypeStruct((t // MOBA_BLOCK, 2 * D_ATTN, MOBA_BLOCK), _BF16),
            jax.ShapeDtypeStruct((t, D_POOL), _BF16),
        ],
        scratch_shapes=[pltpu.VMEM((POOL_HALO + tm, D_POOL), _F32),
                        pltpu.VMEM((2, POOL_HALO + tm, POOL_GROUP), _F32)],
        compiler_params=pltpu.CompilerParams(
            dimension_semantics=("arbitrary",), vmem_limit_bytes=VMEM_LIMIT),
        name="proj_pool",
    )(x2d, g, win, wqv, wpool, pscale)


def _select_blocks(gate, n_past):
    blk = lax.broadcasted_iota(jnp.int32, gate.shape, 0)
    n_blocks = gate.shape[0]
    g = jnp.where(blk < n_past, gate, -jnp.inf)
    sel = jnp.zeros(gate.shape, _F32)
    for _ in range(MOBA_TOPK):
        mx = jnp.max(g, axis=0, keepdims=True)
        is_max = jnp.logical_and(g == mx, mx > -jnp.inf)
        first = jnp.min(jnp.where(is_max, blk, n_blocks), axis=0, keepdims=True)
        pick = blk == first
        sel = jnp.where(pick, 1.0, sel)
        g = jnp.where(pick, -jnp.inf, g)
    return sel


def _moba_kernel(q_ref, k_ref, v_ref, tab_ref, o_ref, kmean_ref, fsel_ref, s_ref, acc_ref,
                 st_ref):
    t = pl.program_id(1)
    nb = v_ref.shape[1]
    tq = q_ref.shape[3]
    lanes = [(sq, a) for sq in range(SEQS_PER_STEP) for a in range(N_HEADS)]
    n_lanes = len(lanes)
    head = lambda a: slice(a * HEAD_DIM, (a + 1) * HEAD_DIM)
    CMAX, RUN_MAX, RUN_SUM, FLAGS = 0, 1, 2, 3

    @pl.when(t == 0)
    def _():
        for sq in range(SEQS_PER_STEP):
            for n in range(nb):
                kb = k_ref[sq, n * MOBA_BLOCK:(n + 1) * MOBA_BLOCK, :].astype(_F32)
                kmean_ref[sq, n:n + 1, :] = (
                    jnp.sum(kb, axis=0, keepdims=True) * (1.0 / MOBA_BLOCK))
        s_ref[...] = jnp.full(s_ref.shape, MASKED, _F32)
        acc_ref[...] = jnp.zeros(acc_ref.shape, _F32)
        st_ref[CMAX] = jnp.full(st_ref.shape[1:], MASKED, _F32)
        st_ref[RUN_MAX] = jnp.full(st_ref.shape[1:], MASKED, _F32)
        st_ref[RUN_SUM] = jnp.ones(st_ref.shape[1:], _F32)
        for r in range(BLOCKS_PER_CHUNK):
            st_ref[FLAGS + r] = jnp.zeros(st_ref.shape[1:], _F32)

    j = jnp.minimum(t, nb - 1)
    prev = jnp.maximum(t - 1, 0)
    q_heads = [q_ref[sq, 0, head(a), :] for sq, a in lanes]

    kmean = [kmean_ref[sq].astype(_BF16) for sq in range(SEQS_PER_STEP)]
    blk = lax.broadcasted_iota(jnp.int32, (nb, tq), 0)
    adj_flag = []
    for c, (sq, a) in enumerate(lanes):
        sel = _select_blocks(_dot(kmean[sq][:, head(a)], q_heads[c]), j)
        is_adj = blk == j - 1
        fsel_ref[c] = jnp.where(is_adj, 0.0, sel)
        flag = jnp.max(jnp.where(is_adj, sel, 0.0), axis=0, keepdims=True)
        adj_flag.append(jnp.where(j == 0, 1.0, flag))

    first = jnp.maximum(j - 1, 0)
    variant = jnp.where(j == 0, 1, 0)
    n_far_chunks = jnp.where(t < nb, lax.shift_right_logical(j, 1), 0)
    prev_far_chunks = lax.shift_right_logical(prev, 1)
    prev_last_block = jnp.where(prev_far_chunks == 0, jnp.maximum(prev - 1, 0),
                                (prev_far_chunks - 1) * BLOCKS_PER_CHUNK)

    def key_chunk(first_row):
        rows = pl.ds(pl.multiple_of(first_row, MOBA_BLOCK), KEY_CHUNK)
        return [k_ref[sq, rows, :] for sq in range(SEQS_PER_STEP)]

    ones = jnp.ones((SUM_ROWS, MOBA_BLOCK), _BF16)

    def block_rows(r):
        return slice(r * MOBA_BLOCK, (r + 1) * MOBA_BLOCK)

    def chunk_max(s, flags):
        cmax = None
        for r in range(BLOCKS_PER_CHUNK):
            bmax = jnp.max(s[block_rows(r)], axis=0, keepdims=True)
            bmax = jnp.where(flags[r] > 0.0, bmax, MASKED)
            cmax = bmax if cmax is None else jnp.maximum(cmax, bmax)
        return cmax

    def set_flags(c, flags):
        for r in range(BLOCKS_PER_CHUNK):
            st_ref[FLAGS + r, c] = flags[r]
        return [st_ref[FLAGS + r, c] for r in range(BLOCKS_PER_CHUNK)]

    def fold_lane(c, first_block, cmax, m, l):
        sq, a = lanes[c]
        m_new = jnp.maximum(m, cmax)
        alpha = jnp.exp2(m - m_new)
        pv = None
        for r in range(BLOCKS_PER_CHUNK):
            offset = jnp.where(st_ref[FLAGS + r, c] > 0.0, -m_new, MASKED)
            p = jnp.exp2(s_ref[c, block_rows(r), :] + offset).astype(_BF16)
            v_t = jnp.concatenate([v_ref[sq, first_block + r, head(a), :], ones], axis=0)
            d = _dot(v_t, p)
            pv = d if pv is None else pv + d
        acc_ref[c] = alpha * acc_ref[c] + pv[:HEAD_DIM]
        return m_new, alpha * l + pv[HEAD_DIM:HEAD_DIM + 1]


    kb = key_chunk(first * MOBA_BLOCK)
    own_score = lambda c: _dot(kb[lanes[c][0]][:, head(lanes[c][1])], q_heads[c])
    score = own_score(0)
    cmax0, outs = [], []
    all_queries = jnp.ones((1, tq), _F32)
    for c, (sq, a) in enumerate(lanes):
        next_score = own_score(c + 1) if c + 1 < n_lanes else None
        _, l_c = fold_lane(c, prev_last_block, st_ref[CMAX, c], st_ref[RUN_MAX, c],
                           st_ref[RUN_SUM, c])
        outs.append(acc_ref[c] / l_c)
        acc_ref[c] = jnp.zeros((HEAD_DIM, tq), _F32)
        s = score + tab_ref[a, variant]
        s_ref[c] = s
        flags = set_flags(c, [adj_flag[c]] + [all_queries] * (BLOCKS_PER_CHUNK - 1))
        cmax0.append(chunk_max(s, flags))
        score = next_score
    for sq in range(SEQS_PER_STEP):
        out_t = jnp.concatenate(outs[sq * N_HEADS:(sq + 1) * N_HEADS], axis=0)
        o_ref[sq] = out_t.T.astype(_BF16)

    def step(i, carry):
        cmax, m, l = carry
        kb = key_chunk(i * KEY_CHUNK)
        score_lane = lambda c: _dot(kb[lanes[c][0]][:, head(lanes[c][1])], q_heads[c])
        first_block = jnp.where(i == 0, first, (i - 1) * BLOCKS_PER_CHUNK)
        new_cmax, new_m, new_l = [], [], []
        score = score_lane(0)
        for c in range(n_lanes):
            next_score = score_lane(c + 1) if c + 1 < n_lanes else None
            m_c, l_c = fold_lane(c, first_block, cmax[c], m[c], l[c])
            s_ref[c] = score
            flags = set_flags(c, [fsel_ref[c, pl.ds(i * BLOCKS_PER_CHUNK + r, 1), :]
                                  for r in range(BLOCKS_PER_CHUNK)])
            new_cmax.append(chunk_max(score, flags))
            new_m.append(m_c)
            new_l.append(l_c)
            score = next_score
        return tuple(new_cmax), tuple(new_m), tuple(new_l)

    init_m = tuple(jnp.full((1, tq), MASKED, _F32) for _ in range(n_lanes))
    init_l = tuple(jnp.zeros((1, tq), _F32) for _ in range(n_lanes))
    cmax, m, l = lax.fori_loop(0, n_far_chunks, step, (tuple(cmax0), init_m, init_l))
    for c in range(n_lanes):
        st_ref[CMAX, c] = cmax[c]
        st_ref[RUN_MAX, c] = m[c]
        st_ref[RUN_SUM, c] = l[c]


def _moba(qv3, k, tables, *, batch, seq_len):
    nb = seq_len // MOBA_BLOCK
    sq = SEQS_PER_STEP
    n_lanes = sq * N_HEADS
    assert nb % BLOCKS_PER_CHUNK == 0 and batch % sq == 0
    qv4 = qv3.reshape(batch, nb, 2 * D_ATTN, MOBA_BLOCK)
    k3 = k.reshape(batch, seq_len, D_ATTN)
    out = pl.pallas_call(
        _moba_kernel,
        grid=(batch // sq, nb + 1),
        in_specs=[
            pl.BlockSpec((sq, 1, D_ATTN, MOBA_BLOCK),
                         lambda g, t: (g, jnp.minimum(t, nb - 1), 0, 0)),
            pl.BlockSpec((sq, seq_len, D_ATTN), lambda g, t: (g, 0, 0)),
            pl.BlockSpec((sq, nb, D_ATTN, MOBA_BLOCK), lambda g, t: (g, 0, 1, 0)),
            pl.BlockSpec((N_HEADS, 2, KEY_CHUNK, MOBA_BLOCK), lambda g, t: (0, 0, 0, 0),
                         pipeline_mode=pl.Buffered(1)),
        ],
        out_specs=pl.BlockSpec((sq, MOBA_BLOCK, D_ATTN),
                               lambda g, t: (g, jnp.maximum(t - 1, 0), 0)),
        out_shape=jax.ShapeDtypeStruct((batch, seq_len, D_ATTN), _BF16),
        scratch_shapes=[pltpu.VMEM((sq, nb, D_ATTN), _F32),
                        pltpu.VMEM((n_lanes, nb, MOBA_BLOCK), _F32),
                        pltpu.VMEM((n_lanes, KEY_CHUNK, MOBA_BLOCK), _F32),
                        pltpu.VMEM((n_lanes, HEAD_DIM, MOBA_BLOCK), _F32),
                        pltpu.VMEM((3 + BLOCKS_PER_CHUNK, n_lanes, 1, MOBA_BLOCK), _F32)],
        compiler_params=pltpu.CompilerParams(
            dimension_semantics=("arbitrary", "arbitrary"),
            vmem_limit_bytes=VMEM_LIMIT),
        name="moba_attention",
    )(qv4, k3, qv4, tables)
    return out.reshape(batch * seq_len, D_ATTN)


def _shift_rows(g, prev_rows, lag):
    rolled = pltpu.roll(g, lag, 0)
    rid = lax.broadcasted_iota(jnp.int32, prev_rows.shape, 0)
    head = jnp.where(rid < lag, pltpu.roll(prev_rows, lag, 0), rolled[:CONV_HALO])
    return jnp.concatenate([head, rolled[CONV_HALO:]], axis=0)


def _ffn_kernel(x_ref, p_ref, ypool_ref, yattn_ref, wo_ref, g_post_ref,
                g_pre_ref, wup_ref, convw_ref, convb_ref, wdown_ref, g_ffn_post_ref,
                wple_ref, wplegate_ref, o_ref, carry_ref, act_ref, *, tiles_per_seq):
    i = pl.program_id(0)
    tm = x_ref.shape[0]

    @pl.when(i % tiles_per_seq == 0)
    def _():
        carry_ref[...] = jnp.zeros(carry_ref.shape, _F32)

    y = _dot(ypool_ref[...], wo_ref[0:D_POOL, :]) + _dot(yattn_ref[...], wo_ref[D_POOL:, :])
    ple = _dot(p_ref[...].astype(_BF16), wple_ref[...])
    x1 = x_ref[...] + _rms(y, g_post_ref[...])
    h = _rms(x1, g_pre_ref[...]).astype(_BF16)

    for c in range(N_FF_CHUNKS):
        cols = slice(c * FF_CHUNK, (c + 1) * FF_CHUNK)
        gate = _dot(h, wup_ref[:, cols])
        val = _dot(h, wup_ref[:, D_FF + c * FF_CHUNK:D_FF + (c + 1) * FF_CHUNK])
        prev = carry_ref[:, cols]
        conv = convb_ref[:, cols] + _shift_rows(gate, prev, 2) * convw_ref[0:1, cols]
        conv = conv + _shift_rows(gate, prev, 1) * convw_ref[1:2, cols]
        conv = conv + gate * convw_ref[2:3, cols]
        carry_ref[:, cols] = gate[tm - CONV_HALO:, :]
        act_ref[:, cols] = (jax.nn.gelu(conv, approximate=True) * val).astype(_BF16)

    x2 = x1 + _rms(_dot(act_ref[...], wdown_ref[...]), g_ffn_post_ref[...])
    gate_logit = _dot(x2.astype(_BF16), wplegate_ref[...])
    o_ref[...] = x2 + ple * (1.0 / (1.0 + jnp.exp(-gate_logit)))


def _ffn(x2d, p3d, ypool, yattn, wo, g_post, g_pre, wup, convw, convb,
         wdown, g_ffn_post, wple, wplegate, *, layer, seq_len):
    t = x2d.shape[0]
    tm = ROW_TILE
    rows = lambda width: pl.BlockSpec((tm, width), lambda i: (i, 0))
    const = functools.partial(_layer_block, layer, pipeline_mode=pl.Buffered(1))
    return pl.pallas_call(
        functools.partial(_ffn_kernel, tiles_per_seq=seq_len // tm),
        grid=(t // tm,),
        in_specs=[
            rows(D_MODEL), pl.BlockSpec((None, tm, PLE_DIM), lambda i: (layer, i, 0)),
            rows(D_POOL), rows(D_ATTN),
            const((D_MODEL, D_MODEL)), const((1, D_MODEL)),
            const((1, D_MODEL)),
            const((D_MODEL, 2 * D_FF)),
            const((CONV_WIDTH, D_FF)), const((1, D_FF)),
            const((D_FF, D_MODEL)), const((1, D_MODEL)),
            const((PLE_DIM, D_MODEL)), const((D_MODEL, D_MODEL)),
        ],
        out_specs=rows(D_MODEL),
        out_shape=jax.ShapeDtypeStruct((t, D_MODEL), _F32),
        scratch_shapes=[pltpu.VMEM((CONV_HALO, D_FF), _F32), pltpu.VMEM((tm, D_FF), _BF16)],
        compiler_params=pltpu.CompilerParams(
            dimension_semantics=("arbitrary",), vmem_limit_bytes=VMEM_LIMIT),
        name="ffn_block",
    )(x2d, p3d, ypool, yattn, wo, g_post, g_pre, wup, convw, convb,
      wdown, g_ffn_post, wple, wplegate)


def kernel(x, p, rel_bias, g_mix_pre, g_mix_post, g_ffn_pre, g_ffn_post, w_in, w_pool,
           pool_scale, w_out, w_up, conv_w, conv_b, w_down, w_ple, w_ple_gate):
    batch, seq_len, d_model = x.shape
    depth = w_in.shape[0]
    assert d_model == D_MODEL and seq_len % ROW_TILE == 0
    assert seq_len % PROJ_ROW_TILE == 0 and PROJ_ROW_TILE % MOBA_BLOCK == 0
    t = batch * seq_len
    rows = lambda v: v.reshape(depth, 1, -1).astype(_F32)
    bf16 = lambda w: w.astype(_BF16)
    w_in_b, w_pool_b, w_out_b, w_up_b, w_down_b = map(bf16, (w_in, w_pool, w_out, w_up, w_down))
    w_qv_t = jnp.concatenate(
        [w_in_b[..., D_POOL:D_POOL + D_ATTN], w_in_b[..., D_POOL + 2 * D_ATTN:]], axis=-1,
    ).transpose(0, 2, 1)
    w_ple_b, w_ple_gate_b = bf16(w_ple), bf16(w_ple_gate)
    p3d = p.reshape(depth, t, PLE_DIM).astype(_F32)

    tables = _bias_tables(rel_bias)
    xf = x.reshape(t, d_model).astype(_F32)
    for i in range(depth):
        k, qv3, ypool = _proj_pool(
            xf, rows(g_mix_pre), w_in_b, w_qv_t, w_pool_b, rows(pool_scale),
            layer=i, seq_len=seq_len)
        yattn = _moba(qv3, k, tables, batch=batch, seq_len=seq_len)
        xf = _ffn(
            xf, p3d, ypool, yattn, w_out_b, rows(g_mix_post), rows(g_ffn_pre), w_up_b,
            conv_w.astype(_F32), rows(conv_b), w_down_b, rows(g_ffn_post), w_ple_b,
            w_ple_gate_b, layer=i, seq_len=seq_len)
    return xf.reshape(batch, seq_len, d_model).astype(x.dtype)
```

```python
import functools
import math

import numpy as np
import jax
import jax.numpy as jnp
from jax import lax
from jax.experimental import pallas as pl
from jax.experimental.pallas import tpu as pltpu

D_MODEL = 1024
PLE_DIM = 256
D_POOL = 512
POOL_WINDOWS = (2, 4, 8, 16)
POOL_GROUP = D_POOL // len(POOL_WINDOWS)
D_ATTN = D_MODEL - D_POOL
N_HEADS = 8
HEAD_DIM = D_ATTN // N_HEADS
MOBA_BLOCK = 256
MOBA_TOPK = 3
NUM_BUCKETS = 32
MAX_DISTANCE = 128
D_FF = 2816
CONV_WIDTH = 3
EPS = 1e-6

SEQS_PER_STEP = 2
POOL_HIST = 16
POOL_HALO = 2 * POOL_HIST
CONV_HALO = 8
FF_CHUNK = 256
N_FF_CHUNKS = D_FF // FF_CHUNK
ROW_TILE = 512
PROJ_ROW_TILE = 1024
MASKED = -1e30
LOG2_E = math.log2(math.e)
BLOCKS_PER_CHUNK = 2
KEY_CHUNK = BLOCKS_PER_CHUNK * MOBA_BLOCK
SUM_ROWS = 16
VMEM_LIMIT = 56 * 1024 * 1024

_BF16 = jnp.bfloat16
_F32 = jnp.float32


def _rms(x, g):
    return x * lax.rsqrt(jnp.mean(x * x, axis=-1, keepdims=True) + EPS) * g


def _dot(a, b):
    return jnp.dot(a, b, preferred_element_type=_F32)


def _bucket_tables():
    key = np.arange(MOBA_BLOCK)[:, None]
    qry = np.arange(MOBA_BLOCK)[None, :]

    def bucket(dist):
        n = np.maximum(dist, 0)
        max_exact = NUM_BUCKETS // 2
        nf = np.maximum(n, 1).astype(np.float32)
        large = max_exact + (np.log(nf / max_exact) / math.log(MAX_DISTANCE / max_exact)
                             * (NUM_BUCKETS - max_exact)).astype(np.int32)
        large = np.minimum(large, NUM_BUCKETS - 1)
        return np.where(n < max_exact, n, large).astype(np.int32)

    d_own = qry - key
    own = np.where(d_own >= 0, bucket(d_own), -1).astype(np.int32)
    adj = bucket(d_own + MOBA_BLOCK)
    return own, adj


def _bias_table_kernel(rb_ref, own_idx_ref, adj_idx_ref, tab_ref):
    h = pl.program_id(0)
    far = rb_ref[NUM_BUCKETS - 1, h]
    own_idx = own_idx_ref[...]
    adj_idx = adj_idx_ref[...]
    own = jnp.zeros(own_idx.shape, _F32)
    adj = jnp.zeros(adj_idx.shape, _F32)
    for b in range(NUM_BUCKETS):
        val = (rb_ref[b, h] - far) * LOG2_E
        own = jnp.where(own_idx == b, val, own)
        adj = jnp.where(adj_idx == b, val, adj)
    own = jnp.where(own_idx < 0, MASKED, own)
    tab_ref[0, 0, 0:MOBA_BLOCK, :] = adj
    tab_ref[0, 0, MOBA_BLOCK:, :] = own
    tab_ref[0, 1, 0:MOBA_BLOCK, :] = own
    tab_ref[0, 1, MOBA_BLOCK:, :] = jnp.full(own.shape, MASKED, _F32)


def _bias_tables(rel_bias):
    own_idx, adj_idx = _bucket_tables()
    blk = pl.BlockSpec((MOBA_BLOCK, MOBA_BLOCK), lambda h: (0, 0))
    return pl.pallas_call(
        _bias_table_kernel,
        grid=(N_HEADS,),
        in_specs=[pl.BlockSpec(memory_space=pltpu.SMEM), blk, blk],
        out_specs=pl.BlockSpec((1, 2, KEY_CHUNK, MOBA_BLOCK), lambda h: (h, 0, 0, 0)),
        out_shape=jax.ShapeDtypeStruct((N_HEADS, 2, KEY_CHUNK, MOBA_BLOCK), _F32),
        name="bias_tables",
    )(rel_bias.astype(_F32), jnp.asarray(own_idx), jnp.asarray(adj_idx))


def _proj_pool_kernel(x_ref, g_ref, win_ref, wpool_ref, pscale_ref,
                      k_ref, qv_ref, ypool_ref, ext_ref, lvl_ref, *, tiles_per_seq):
    i = pl.program_id(0)
    tm = x_ref.shape[0]
    h = _rms(x_ref[...], g_ref[...]).astype(_BF16)
    seq_tile = i % tiles_per_seq

    @pl.when(seq_tile == 0)
    def _():
        ext_ref[0:POOL_HALO, :] = jnp.zeros((POOL_HALO, D_POOL), _F32)

    u = _dot(h, win_ref[:, 0:D_POOL])
    ext_ref[POOL_HALO:POOL_HALO + tm, :] = u

    k_ref[...] = _dot(h, win_ref[:, D_POOL + D_ATTN:D_POOL + 2 * D_ATTN]).astype(_BF16)

    q_t = (_dot(h, win_ref[:, D_POOL:D_POOL + D_ATTN]) * (HEAD_DIM ** -0.5 * LOG2_E)).T
    v_t = _dot(h, win_ref[:, D_POOL + 2 * D_ATTN:]).T
    for c in range(tm // MOBA_BLOCK):
        cols = slice(c * MOBA_BLOCK, (c + 1) * MOBA_BLOCK)
        qv_ref[c, 0:D_ATTN, :] = q_t[:, cols].astype(_BF16)
        qv_ref[c, D_ATTN:, :] = v_t[:, cols].astype(_BF16)

    end = POOL_HALO + tm
    pos = seq_tile * tm + lax.broadcasted_iota(jnp.int32, (tm, 1), 0)
    for g, w in enumerate(POOL_WINDOWS):
        cols = slice(g * POOL_GROUP, (g + 1) * POOL_GROUP)
        levels = w.bit_length() - 1
        assert w == 1 << levels and 1 <= levels <= 4
        read = lambda lo, hi: ext_ref[lo:hi, cols]
        for lvl in range(1, levels + 1):
            delay = 1 << (lvl - 1)
            lo = POOL_HALO if lvl == levels else 8 * lvl
            acc = read(lo, end) + read(lo - delay, end - delay)
            if lvl < levels:
                slot = lvl % 2
                lvl_ref[slot, lo:end, :] = acc
                read = lambda lo, hi, slot=slot: lvl_ref[slot, lo:hi, :]
        u_g = u[:, cols]
        count = jnp.minimum(pos + 1, w).astype(_F32)
        pooled = (acc / count - u_g).astype(_BF16)
        mixed = _dot(pooled, wpool_ref[g]) * pscale_ref[:, cols]
        ypool_ref[:, cols] = mixed.astype(_BF16)
    ext_ref[POOL_HALO - POOL_HIST:POOL_HALO, :] = u[tm - POOL_HIST:, :]


def _layer_block(layer, shape, **kwargs):
    return pl.BlockSpec((None,) + shape, lambda i: (layer,) + (0,) * len(shape), **kwargs)


def _proj_pool(x2d, g, win, wpool, pscale, *, layer, seq_len):
    t = x2d.shape[0]
    tm = PROJ_ROW_TILE
    const = functools.partial(_layer_block, layer)
    return pl.pallas_call(
        functools.partial(_proj_pool_kernel, tiles_per_seq=seq_len // tm),
        grid=(t // tm,),
        in_specs=[
            pl.BlockSpec((tm, D_MODEL), lambda i: (i, 0)),
            const((1, D_MODEL)),
            const((D_MODEL, D_POOL + 3 * D_ATTN)),
            const((len(POOL_WINDOWS), POOL_GROUP, POOL_GROUP)),
            const((1, D_POOL)),
        ],
        out_specs=[
            pl.BlockSpec((tm, D_ATTN), lambda i: (i, 0)),
            pl.BlockSpec((tm // MOBA_BLOCK, 2 * D_ATTN, MOBA_BLOCK), lambda i: (i, 0, 0)),
            pl.BlockSpec((tm, D_POOL), lambda i: (i, 0)),
        ],
        out_shape=[
            jax.ShapeDtypeStruct((t, D_ATTN), _BF16),
            jax.ShapeDtypeStruct((t // MOBA_BLOCK, 2 * D_ATTN, MOBA_BLOCK), _BF16),
            jax.ShapeDtypeStruct((t, D_POOL), _BF16),
        ],
        scratch_shapes=[pltpu.VMEM((POOL_HALO + tm, D_POOL), _F32),
                        pltpu.VMEM((2, POOL_HALO + tm, POOL_GROUP), _F32)],
        compiler_params=pltpu.CompilerParams(
            dimension_semantics=("arbitrary",), vmem_limit_bytes=VMEM_LIMIT),
        name="proj_pool",
    )(x2d, g, win, wpool, pscale)


def _select_blocks(gate, n_past):
    blk = lax.broadcasted_iota(jnp.int32, gate.shape, 0)
    n_blocks = gate.shape[0]
    g = jnp.where(blk < n_past, gate, -jnp.inf)
    sel = jnp.zeros(gate.shape, _F32)
    for _ in range(MOBA_TOPK):
        mx = jnp.max(g, axis=0, keepdims=True)
        is_max = jnp.logical_and(g == mx, mx > -jnp.inf)
        first = jnp.min(jnp.where(is_max, blk, n_blocks), axis=0, keepdims=True)
        pick = blk == first
        sel = jnp.where(pick, 1.0, sel)
        g = jnp.where(pick, -jnp.inf, g)
    return sel


def _moba_kernel(q_ref, k_ref, v_ref, tab_ref, o_ref, kmean_ref, fsel_ref, s_ref, acc_ref,
                 st_ref):
    t = pl.program_id(1)
    nb = v_ref.shape[1]
    tq = q_ref.shape[3]
    lanes = [(sq, a) for sq in range(SEQS_PER_STEP) for a in range(N_HEADS)]
    n_lanes = len(lanes)
    head = lambda a: slice(a * HEAD_DIM, (a + 1) * HEAD_DIM)
    CMAX, RUN_MAX, RUN_SUM, FLAGS = 0, 1, 2, 3

    @pl.when(t == 0)
    def _():
        for sq in range(SEQS_PER_STEP):
            for n in range(nb):
                kb = k_ref[sq, n * MOBA_BLOCK:(n + 1) * MOBA_BLOCK, :].astype(_F32)
                kmean_ref[sq, n:n + 1, :] = (
                    jnp.sum(kb, axis=0, keepdims=True) * (1.0 / MOBA_BLOCK))
        s_ref[...] = jnp.full(s_ref.shape, MASKED, _F32)
        acc_ref[...] = jnp.zeros(acc_ref.shape, _F32)
        st_ref[CMAX] = jnp.full(st_ref.shape[1:], MASKED, _F32)
        st_ref[RUN_MAX] = jnp.full(st_ref.shape[1:], MASKED, _F32)
        st_ref[RUN_SUM] = jnp.ones(st_ref.shape[1:], _F32)
        for r in range(BLOCKS_PER_CHUNK):
            st_ref[FLAGS + r] = jnp.zeros(st_ref.shape[1:], _F32)

    j = jnp.minimum(t, nb - 1)
    prev = jnp.maximum(t - 1, 0)
    q_heads = [q_ref[sq, 0, head(a), :] for sq, a in lanes]

    kmean = [kmean_ref[sq].astype(_BF16) for sq in range(SEQS_PER_STEP)]
    blk = lax.broadcasted_iota(jnp.int32, (nb, tq), 0)
    adj_flag = []
    for c, (sq, a) in enumerate(lanes):
        sel = _select_blocks(_dot(kmean[sq][:, head(a)], q_heads[c]), j)
        is_adj = blk == j - 1
        fsel_ref[c] = jnp.where(is_adj, 0.0, sel)
        flag = jnp.max(jnp.where(is_adj, sel, 0.0), axis=0, keepdims=True)
        adj_flag.append(jnp.where(j == 0, 1.0, flag))

    first = jnp.maximum(j - 1, 0)
    variant = jnp.where(j == 0, 1, 0)
    n_far_chunks = jnp.where(t < nb, lax.shift_right_logical(j, 1), 0)
    prev_far_chunks = lax.shift_right_logical(prev, 1)
    prev_last_block = jnp.where(prev_far_chunks == 0, jnp.maximum(prev - 1, 0),
                                (prev_far_chunks - 1) * BLOCKS_PER_CHUNK)

    def key_chunk(first_row):
        rows = pl.ds(pl.multiple_of(first_row, MOBA_BLOCK), KEY_CHUNK)
        return [k_ref[sq, rows, :] for sq in range(SEQS_PER_STEP)]

    ones = jnp.ones((SUM_ROWS, MOBA_BLOCK), _BF16)

    def block_rows(r):
        return slice(r * MOBA_BLOCK, (r + 1) * MOBA_BLOCK)

    def chunk_max(s, flags):
        cmax = None
        for r in range(BLOCKS_PER_CHUNK):
            bmax = jnp.max(s[block_rows(r)], axis=0, keepdims=True)
            bmax = jnp.where(flags[r] > 0.0, bmax, MASKED)
            cmax = bmax if cmax is None else jnp.maximum(cmax, bmax)
        return cmax

    def set_flags(c, flags):
        for r in range(BLOCKS_PER_CHUNK):
            st_ref[FLAGS + r, c] = flags[r]
        return [st_ref[FLAGS + r, c] for r in range(BLOCKS_PER_CHUNK)]

    def fold_lane(c, first_block, cmax, m, l):
        sq, a = lanes[c]
        m_new = jnp.maximum(m, cmax)
        alpha = jnp.exp2(m - m_new)
        pv = None
        for r in range(BLOCKS_PER_CHUNK):
            offset = jnp.where(st_ref[FLAGS + r, c] > 0.0, -m_new, MASKED)
            p = jnp.exp2(s_ref[c, block_rows(r), :] + offset).astype(_BF16)
            v_t = jnp.concatenate([v_ref[sq, first_block + r, head(a), :], ones], axis=0)
            d = _dot(v_t, p)
            pv = d if pv is None else pv + d
        acc_ref[c] = alpha * acc_ref[c] + pv[:HEAD_DIM]
        return m_new, alpha * l + pv[HEAD_DIM:HEAD_DIM + 1]


    kb = key_chunk(first * MOBA_BLOCK)
    own_score = lambda c: _dot(kb[lanes[c][0]][:, head(lanes[c][1])], q_heads[c])
    score = own_score(0)
    cmax0, outs = [], []
    all_queries = jnp.ones((1, tq), _F32)
    for c, (sq, a) in enumerate(lanes):
        next_score = own_score(c + 1) if c + 1 < n_lanes else None
        _, l_c = fold_lane(c, prev_last_block, st_ref[CMAX, c], st_ref[RUN_MAX, c],
                           st_ref[RUN_SUM, c])
        outs.append(acc_ref[c] / l_c)
        acc_ref[c] = jnp.zeros((HEAD_DIM, tq), _F32)
        s = score + tab_ref[a, variant]
        s_ref[c] = s
        flags = set_flags(c, [adj_flag[c]] + [all_queries] * (BLOCKS_PER_CHUNK - 1))
        cmax0.append(chunk_max(s, flags))
        score = next_score
    for sq in range(SEQS_PER_STEP):
        out_t = jnp.concatenate(outs[sq * N_HEADS:(sq + 1) * N_HEADS], axis=0)
        o_ref[sq] = out_t.T.astype(_BF16)

    def step(i, carry):
        cmax, m, l = carry
        kb = key_chunk(i * KEY_CHUNK)
        score_lane = lambda c: _dot(kb[lanes[c][0]][:, head(lanes[c][1])], q_heads[c])
        first_block = jnp.where(i == 0, first, (i - 1) * BLOCKS_PER_CHUNK)
        new_cmax, new_m, new_l = [], [], []
        score = score_lane(0)
        for c in range(n_lanes):
            next_score = score_lane(c + 1) if c + 1 < n_lanes else None
            m_c, l_c = fold_lane(c, first_block, cmax[c], m[c], l[c])
            s_ref[c] = score
            flags = set_flags(c, [fsel_ref[c, pl.ds(i * BLOCKS_PER_CHUNK + r, 1), :]
                                  for r in range(BLOCKS_PER_CHUNK)])
            new_cmax.append(chunk_max(score, flags))
            new_m.append(m_c)
            new_l.append(l_c)
            score = next_score
        return tuple(new_cmax), tuple(new_m), tuple(new_l)

    init_m = tuple(jnp.full((1, tq), MASKED, _F32) for _ in range(n_lanes))
    init_l = tuple(jnp.zeros((1, tq), _F32) for _ in range(n_lanes))
    cmax, m, l = lax.fori_loop(0, n_far_chunks, step, (tuple(cmax0), init_m, init_l))
    for c in range(n_lanes):
        st_ref[CMAX, c] = cmax[c]
        st_ref[RUN_MAX, c] = m[c]
        st_ref[RUN_SUM, c] = l[c]


def _moba(qv3, k, tables, *, batch, seq_len):
    nb = seq_len // MOBA_BLOCK
    sq = SEQS_PER_STEP
    n_lanes = sq * N_HEADS
    assert nb % BLOCKS_PER_CHUNK == 0 and batch % sq == 0
    qv4 = qv3.reshape(batch, nb, 2 * D_ATTN, MOBA_BLOCK)
    k3 = k.reshape(batch, seq_len, D_ATTN)
    out = pl.pallas_call(
        _moba_kernel,
        grid=(batch // sq, nb + 1),
        in_specs=[
            pl.BlockSpec((sq, 1, D_ATTN, MOBA_BLOCK),
                         lambda g, t: (g, jnp.minimum(t, nb - 1), 0, 0)),
            pl.BlockSpec((sq, seq_len, D_ATTN), lambda g, t: (g, 0, 0)),
            pl.BlockSpec((sq, nb, D_ATTN, MOBA_BLOCK), lambda g, t: (g, 0, 1, 0)),
            pl.BlockSpec((N_HEADS, 2, KEY_CHUNK, MOBA_BLOCK), lambda g, t: (0, 0, 0, 0),
                         pipeline_mode=pl.Buffered(1)),
        ],
        out_specs=pl.BlockSpec((sq, MOBA_BLOCK, D_ATTN),
                               lambda g, t: (g, jnp.maximum(t - 1, 0), 0)),
        out_shape=jax.ShapeDtypeStruct((batch, seq_len, D_ATTN), _BF16),
        scratch_shapes=[pltpu.VMEM((sq, nb, D_ATTN), _F32),
                        pltpu.VMEM((n_lanes, nb, MOBA_BLOCK), _F32),
                        pltpu.VMEM((n_lanes, KEY_CHUNK, MOBA_BLOCK), _F32),
                        pltpu.VMEM((n_lanes, HEAD_DIM, MOBA_BLOCK), _F32),
                        pltpu.VMEM((3 + BLOCKS_PER_CHUNK, n_lanes, 1, MOBA_BLOCK), _F32)],
        compiler_params=pltpu.CompilerParams(
            dimension_semantics=("arbitrary", "arbitrary"),
            vmem_limit_bytes=VMEM_LIMIT),
        name="moba_attention",
    )(qv4, k3, qv4, tables)
    return out.reshape(batch * seq_len, D_ATTN)


def _shift_rows(g, prev_rows, lag):
    rolled = pltpu.roll(g, lag, 0)
    rid = lax.broadcasted_iota(jnp.int32, prev_rows.shape, 0)
    head = jnp.where(rid < lag, pltpu.roll(prev_rows, lag, 0), rolled[:CONV_HALO])
    return jnp.concatenate([head, rolled[CONV_HALO:]], axis=0)


def _ffn_kernel(x_ref, p_ref, ypool_ref, yattn_ref, wo_ref, g_post_ref,
                g_pre_ref, wup_ref, convw_ref, convb_ref, wdown_ref, g_ffn_post_ref,
                wple_ref, wplegate_ref, o_ref, carry_ref, act_ref, *, tiles_per_seq):
    i = pl.program_id(0)
    tm = x_ref.shape[0]

    @pl.when(i % tiles_per_seq == 0)
    def _():
        carry_ref[...] = jnp.zeros(carry_ref.shape, _F32)

    y = _dot(ypool_ref[...], wo_ref[0:D_POOL, :]) + _dot(yattn_ref[...], wo_ref[D_POOL:, :])
    ple = _dot(p_ref[...].astype(_BF16), wple_ref[...])
    x1 = x_ref[...] + _rms(y, g_post_ref[...])
    h = _rms(x1, g_pre_ref[...]).astype(_BF16)

    for c in range(N_FF_CHUNKS):
        cols = slice(c * FF_CHUNK, (c + 1) * FF_CHUNK)
        gate = _dot(h, wup_ref[:, cols])
        val = _dot(h, wup_ref[:, D_FF + c * FF_CHUNK:D_FF + (c + 1) * FF_CHUNK])
        prev = carry_ref[:, cols]
        conv = convb_ref[:, cols] + _shift_rows(gate, prev, 2) * convw_ref[0:1, cols]
        conv = conv + _shift_rows(gate, prev, 1) * convw_ref[1:2, cols]
        conv = conv + gate * convw_ref[2:3, cols]
        carry_ref[:, cols] = gate[tm - CONV_HALO:, :]
        act_ref[:, cols] = (jax.nn.gelu(conv, approximate=True) * val).astype(_BF16)

    x2 = x1 + _rms(_dot(act_ref[...], wdown_ref[...]), g_ffn_post_ref[...])
    gate_logit = _dot(x2.astype(_BF16), wplegate_ref[...])
    o_ref[...] = x2 + ple * (1.0 / (1.0 + jnp.exp(-gate_logit)))


def _ffn(x2d, p3d, ypool, yattn, wo, g_post, g_pre, wup, convw, convb,
         wdown, g_ffn_post, wple, wplegate, *, layer, seq_len):
    t = x2d.shape[0]
    tm = ROW_TILE
    rows = lambda width: pl.BlockSpec((tm, width), lambda i: (i, 0))
    const = functools.partial(_layer_block, layer, pipeline_mode=pl.Buffered(1))
    return pl.pallas_call(
        functools.partial(_ffn_kernel, tiles_per_seq=seq_len // tm),
        grid=(t // tm,),
        in_specs=[
            rows(D_MODEL), pl.BlockSpec((None, tm, PLE_DIM), lambda i: (layer, i, 0)),
            rows(D_POOL), rows(D_ATTN),
            const((D_MODEL, D_MODEL)), const((1, D_MODEL)),
            const((1, D_MODEL)),
            const((D_MODEL, 2 * D_FF)),
            const((CONV_WIDTH, D_FF)), const((1, D_FF)),
            const((D_FF, D_MODEL)), const((1, D_MODEL)),
            const((PLE_DIM, D_MODEL)), const((D_MODEL, D_MODEL)),
        ],
        out_specs=rows(D_MODEL),
        out_shape=jax.ShapeDtypeStruct((t, D_MODEL), _F32),
        scratch_shapes=[pltpu.VMEM((CONV_HALO, D_FF), _F32), pltpu.VMEM((tm, D_FF), _BF16)],
        compiler_params=pltpu.CompilerParams(
            dimension_semantics=("arbitrary",), vmem_limit_bytes=VMEM_LIMIT),
        name="ffn_block",
    )(x2d, p3d, ypool, yattn, wo, g_post, g_pre, wup, convw, convb,
      wdown, g_ffn_post, wple, wplegate)


def kernel(x, p, rel_bias, g_mix_pre, g_mix_post, g_ffn_pre, g_ffn_post, w_in, w_pool,
           pool_scale, w_out, w_up, conv_w, conv_b, w_down, w_ple, w_ple_gate):
    batch, seq_len, d_model = x.shape
    depth = w_in.shape[0]
    assert d_model == D_MODEL and seq_len % ROW_TILE == 0
    assert seq_len % PROJ_ROW_TILE == 0 and PROJ_ROW_TILE % MOBA_BLOCK == 0
    t = batch * seq_len
    rows = lambda v: v.reshape(depth, 1, -1).astype(_F32)
    bf16 = lambda w: w.astype(_BF16)
    w_in_b, w_pool_b, w_out_b, w_up_b, w_down_b = map(bf16, (w_in, w_pool, w_out, w_up, w_down))
    w_ple_b, w_ple_gate_b = bf16(w_ple), bf16(w_ple_gate)
    p3d = p.reshape(depth, t, PLE_DIM).astype(_F32)

    tables = _bias_tables(rel_bias)
    xf = x.reshape(t, d_model).astype(_F32)
    for i in range(depth):
        k, qv3, ypool = _proj_pool(
            xf, rows(g_mix_pre), w_in_b, w_pool_b, rows(pool_scale),
            layer=i, seq_len=seq_len)
        yattn = _moba(qv3, k, tables, batch=batch, seq_len=seq_len)
        xf = _ffn(
            xf, p3d, ypool, yattn, w_out_b, rows(g_mix_post), rows(g_ffn_pre), w_up_b,
            conv_w.astype(_F32), rows(conv_b), w_down_b, rows(g_ffn_post), w_ple_b,
            w_ple_gate_b, layer=i, seq_len=seq_len)
    return xf.reshape(batch, seq_len, d_model).astype(x.dtype)
```

```python
import functools
import math

import numpy as np
import jax
import jax.numpy as jnp
from jax import lax
from jax.experimental import pallas as pl
from jax.experimental.pallas import tpu as pltpu

D_MODEL = 1024
PLE_DIM = 256
D_POOL = 512
POOL_WINDOWS = (2, 4, 8, 16)
POOL_GROUP = D_POOL // len(POOL_WINDOWS)
D_ATTN = D_MODEL - D_POOL
N_HEADS = 8
HEAD_DIM = D_ATTN // N_HEADS
MOBA_BLOCK = 256
MOBA_TOPK = 3
NUM_BUCKETS = 32
MAX_DISTANCE = 128
D_FF = 2816
CONV_WIDTH = 3
EPS = 1e-6

SEQS_PER_STEP = 2
POOL_HIST = 16
POOL_HALO = 2 * POOL_HIST
CONV_HALO = 8
FF_CHUNK = 256
N_FF_CHUNKS = D_FF // FF_CHUNK
ROW_TILE = 512
PROJ_ROW_TILE = 1024
MASKED = -(2.0 ** 100)
LOG2_E = math.log2(math.e)
BLOCKS_PER_CHUNK = 2
KEY_CHUNK = BLOCKS_PER_CHUNK * MOBA_BLOCK
SUM_ROWS = 16
VMEM_LIMIT = 56 * 1024 * 1024

_BF16 = jnp.bfloat16
_F32 = jnp.float32


def _rms(x, g):
    return x * lax.rsqrt(jnp.mean(x * x, axis=-1, keepdims=True) + EPS) * g


def _dot(a, b):
    return jnp.dot(a, b, preferred_element_type=_F32)


def _bucket_tables():
    key = np.arange(MOBA_BLOCK)[:, None]
    qry = np.arange(MOBA_BLOCK)[None, :]

    def bucket(dist):
        n = np.maximum(dist, 0)
        max_exact = NUM_BUCKETS // 2
        nf = np.maximum(n, 1).astype(np.float32)
        large = max_exact + (np.log(nf / max_exact) / math.log(MAX_DISTANCE / max_exact)
                             * (NUM_BUCKETS - max_exact)).astype(np.int32)
        large = np.minimum(large, NUM_BUCKETS - 1)
        return np.where(n < max_exact, n, large).astype(np.int32)

    d_own = qry - key
    own = np.where(d_own >= 0, bucket(d_own), -1).astype(np.int32)
    adj = bucket(d_own + MOBA_BLOCK)
    return own, adj


def _bias_table_kernel(rb_ref, own_idx_ref, adj_idx_ref, tab_ref):
    h = pl.program_id(0)
    far = rb_ref[NUM_BUCKETS - 1, h]
    own_idx = own_idx_ref[...]
    adj_idx = adj_idx_ref[...]
    own = jnp.zeros(own_idx.shape, _F32)
    adj = jnp.zeros(adj_idx.shape, _F32)
    for b in range(NUM_BUCKETS):
        val = (rb_ref[b, h] - far) * LOG2_E
        own = jnp.where(own_idx == b, val, own)
        adj = jnp.where(adj_idx == b, val, adj)
    own = jnp.where(own_idx < 0, MASKED, own)
    tab_ref[0, 0, 0:MOBA_BLOCK, :] = adj
    tab_ref[0, 0, MOBA_BLOCK:, :] = own
    tab_ref[0, 1, 0:MOBA_BLOCK, :] = own
    tab_ref[0, 1, MOBA_BLOCK:, :] = jnp.full(own.shape, MASKED, _F32)


def _bias_tables(rel_bias):
    own_idx, adj_idx = _bucket_tables()
    blk = pl.BlockSpec((MOBA_BLOCK, MOBA_BLOCK), lambda h: (0, 0))
    return pl.pallas_call(
        _bias_table_kernel,
        grid=(N_HEADS,),
        in_specs=[pl.BlockSpec(memory_space=pltpu.SMEM), blk, blk],
        out_specs=pl.BlockSpec((1, 2, KEY_CHUNK, MOBA_BLOCK), lambda h: (h, 0, 0, 0)),
        out_shape=jax.ShapeDtypeStruct((N_HEADS, 2, KEY_CHUNK, MOBA_BLOCK), _F32),
        name="bias_tables",
    )(rel_bias.astype(_F32), jnp.asarray(own_idx), jnp.asarray(adj_idx))


def _proj_pool_kernel(x_ref, g_ref, win_ref, wpool_ref, pscale_ref,
                      k_ref, qv_ref, ypool_ref, ext_ref, lvl_ref, *, tiles_per_seq):
    i = pl.program_id(0)
    tm = x_ref.shape[0]
    h = _rms(x_ref[...], g_ref[...]).astype(_BF16)
    seq_tile = i % tiles_per_seq

    @pl.when(seq_tile == 0)
    def _():
        ext_ref[0:POOL_HALO, :] = jnp.zeros((POOL_HALO, D_POOL), _F32)

    u = _dot(h, win_ref[:, 0:D_POOL])
    ext_ref[POOL_HALO:POOL_HALO + tm, :] = u

    k_ref[...] = _dot(h, win_ref[:, D_POOL + D_ATTN:D_POOL + 2 * D_ATTN]).astype(_BF16)

    q_t = (_dot(h, win_ref[:, D_POOL:D_POOL + D_ATTN]) * (HEAD_DIM ** -0.5 * LOG2_E)).T
    v_t = _dot(h, win_ref[:, D_POOL + 2 * D_ATTN:]).T
    for c in range(tm // MOBA_BLOCK):
        cols = slice(c * MOBA_BLOCK, (c + 1) * MOBA_BLOCK)
        qv_ref[c, 0:D_ATTN, :] = q_t[:, cols].astype(_BF16)
        qv_ref[c, D_ATTN:, :] = v_t[:, cols].astype(_BF16)

    end = POOL_HALO + tm
    pos = seq_tile * tm + lax.broadcasted_iota(jnp.int32, (tm, 1), 0)
    for g, w in enumerate(POOL_WINDOWS):
        cols = slice(g * POOL_GROUP, (g + 1) * POOL_GROUP)
        levels = w.bit_length() - 1
        assert w == 1 << levels and 1 <= levels <= 4
        read = lambda lo, hi: ext_ref[lo:hi, cols]
        for lvl in range(1, levels + 1):
            delay = 1 << (lvl - 1)
            lo = POOL_HALO if lvl == levels else 8 * lvl
            acc = read(lo, end) + read(lo - delay, end - delay)
            if lvl < levels:
                slot = lvl % 2
                lvl_ref[slot, lo:end, :] = acc
                read = lambda lo, hi, slot=slot: lvl_ref[slot, lo:hi, :]
        u_g = u[:, cols]
        count = jnp.minimum(pos + 1, w).astype(_F32)
        pooled = (acc / count - u_g).astype(_BF16)
        mixed = _dot(pooled, wpool_ref[g]) * pscale_ref[:, cols]
        ypool_ref[:, cols] = mixed.astype(_BF16)
    ext_ref[POOL_HALO - POOL_HIST:POOL_HALO, :] = u[tm - POOL_HIST:, :]


def _layer_block(layer, shape, **kwargs):
    return pl.BlockSpec((None,) + shape, lambda i: (layer,) + (0,) * len(shape), **kwargs)


def _proj_pool(x2d, g, win, wpool, pscale, *, layer, seq_len):
    t = x2d.shape[0]
    tm = PROJ_ROW_TILE
    const = functools.partial(_layer_block, layer)
    return pl.pallas_call(
        functools.partial(_proj_pool_kernel, tiles_per_seq=seq_len // tm),
        grid=(t // tm,),
        in_specs=[
            pl.BlockSpec((tm, D_MODEL), lambda i: (i, 0)),
            const((1, D_MODEL)),
            const((D_MODEL, D_POOL + 3 * D_ATTN)),
            const((len(POOL_WINDOWS), POOL_GROUP, POOL_GROUP)),
            const((1, D_POOL)),
        ],
        out_specs=[
            pl.BlockSpec((tm, D_ATTN), lambda i: (i, 0)),
            pl.BlockSpec((tm // MOBA_BLOCK, 2 * D_ATTN, MOBA_BLOCK), lambda i: (i, 0, 0)),
            pl.BlockSpec((tm, D_POOL), lambda i: (i, 0)),
        ],
        out_shape=[
            jax.ShapeDtypeStruct((t, D_ATTN), _BF16),
            jax.ShapeDtypeStruct((t // MOBA_BLOCK, 2 * D_ATTN, MOBA_BLOCK), _BF16),
            jax.ShapeDtypeStruct((t, D_POOL), _BF16),
        ],
        scratch_shapes=[pltpu.VMEM((POOL_HALO + tm, D_POOL), _F32),
                        pltpu.VMEM((2, POOL_HALO + tm, POOL_GROUP), _F32)],
        compiler_params=pltpu.CompilerParams(
            dimension_semantics=("arbitrary",), vmem_limit_bytes=VMEM_LIMIT),
        name="proj_pool",
    )(x2d, g, win, wpool, pscale)


def _select_blocks(gate, n_past):
    blk = lax.broadcasted_iota(jnp.int32, gate.shape, 0)
    n_blocks = gate.shape[0]
    g = jnp.where(blk < n_past, gate, -jnp.inf)
    sel = jnp.zeros(gate.shape, _F32)
    for _ in range(MOBA_TOPK):
        mx = jnp.max(g, axis=0, keepdims=True)
        is_max = jnp.logical_and(g == mx, mx > -jnp.inf)
        first = jnp.min(jnp.where(is_max, blk, n_blocks), axis=0, keepdims=True)
        pick = blk == first
        sel = jnp.where(pick, 1.0, sel)
        g = jnp.where(pick, -jnp.inf, g)
    return sel


def _moba_kernel(q_ref, k_ref, v_ref, tab_ref, o_ref, kmean_ref, fsel_ref, s_ref, acc_ref,
                 st_ref):
    t = pl.program_id(1)
    nb = v_ref.shape[1]
    tq = q_ref.shape[3]
    lanes = [(sq, a) for sq in range(SEQS_PER_STEP) for a in range(N_HEADS)]
    n_lanes = len(lanes)
    head = lambda a: slice(a * HEAD_DIM, (a + 1) * HEAD_DIM)
    CMAX, RUN_MAX, RUN_SUM, FLAGS = 0, 1, 2, 3

    @pl.when(t == 0)
    def _():
        for sq in range(SEQS_PER_STEP):
            for n in range(nb):
                kb = k_ref[sq, n * MOBA_BLOCK:(n + 1) * MOBA_BLOCK, :].astype(_F32)
                kmean_ref[sq, n:n + 1, :] = (
                    jnp.sum(kb, axis=0, keepdims=True) * (1.0 / MOBA_BLOCK))
        s_ref[...] = jnp.full(s_ref.shape, MASKED, s_ref.dtype)
        acc_ref[...] = jnp.zeros(acc_ref.shape, _F32)
        st_ref[CMAX] = jnp.full(st_ref.shape[1:], MASKED, _F32)
        st_ref[RUN_MAX] = jnp.full(st_ref.shape[1:], MASKED, _F32)
        st_ref[RUN_SUM] = jnp.ones(st_ref.shape[1:], _F32)
        for r in range(BLOCKS_PER_CHUNK):
            st_ref[FLAGS + r] = jnp.zeros(st_ref.shape[1:], _F32)

    j = jnp.minimum(t, nb - 1)
    prev = jnp.maximum(t - 1, 0)
    q_heads = [q_ref[sq, 0, head(a), :] for sq, a in lanes]

    kmean = [kmean_ref[sq].astype(_BF16) for sq in range(SEQS_PER_STEP)]
    blk = lax.broadcasted_iota(jnp.int32, (nb, tq), 0)
    adj_flag = []
    for c, (sq, a) in enumerate(lanes):
        sel = _select_blocks(_dot(kmean[sq][:, head(a)], q_heads[c]), j)
        is_adj = blk == j - 1
        fsel_ref[c] = jnp.where(is_adj, 0.0, sel)
        flag = jnp.max(jnp.where(is_adj, sel, 0.0), axis=0, keepdims=True)
        adj_flag.append(jnp.where(j == 0, 1.0, flag))

    first = jnp.maximum(j - 1, 0)
    variant = jnp.where(j == 0, 1, 0)
    n_far_chunks = jnp.where(t < nb, lax.shift_right_logical(j, 1), 0)
    prev_far_chunks = lax.shift_right_logical(prev, 1)
    prev_last_block = jnp.where(prev_far_chunks == 0, jnp.maximum(prev - 1, 0),
                                (prev_far_chunks - 1) * BLOCKS_PER_CHUNK)

    def key_chunk(first_row):
        rows = pl.ds(pl.multiple_of(first_row, MOBA_BLOCK), KEY_CHUNK)
        return [k_ref[sq, rows, :] for sq in range(SEQS_PER_STEP)]

    ones = jnp.ones((SUM_ROWS, MOBA_BLOCK), _BF16)

    def block_rows(r):
        return slice(r * MOBA_BLOCK, (r + 1) * MOBA_BLOCK)

    def chunk_max(s, flags):
        cmax = None
        for r in range(BLOCKS_PER_CHUNK):
            bmax = jnp.max(s[block_rows(r)], axis=0, keepdims=True)
            bmax = jnp.where(flags[r] > 0.0, bmax, MASKED)
            cmax = bmax if cmax is None else jnp.maximum(cmax, bmax)
        return cmax

    def set_flags(c, flags):
        for r in range(BLOCKS_PER_CHUNK):
            st_ref[FLAGS + r, c] = flags[r]
        return [st_ref[FLAGS + r, c] for r in range(BLOCKS_PER_CHUNK)]

    def fold_lane(c, first_block, cmax, m, l):
        sq, a = lanes[c]
        m_new = jnp.maximum(m, cmax).astype(_BF16).astype(_F32)
        alpha = jnp.exp2(m - m_new)
        pv = None
        for r in range(BLOCKS_PER_CHUNK):
            offset = jnp.where(st_ref[FLAGS + r, c] > 0.0, -m_new, MASKED)
            p = jnp.exp2(s_ref[c, block_rows(r), :] + offset.astype(_BF16))
            v_t = jnp.concatenate([v_ref[sq, first_block + r, head(a), :], ones], axis=0)
            d = _dot(v_t, p)
            pv = d if pv is None else pv + d
        acc_ref[c] = alpha * acc_ref[c] + pv[:HEAD_DIM]
        return m_new, alpha * l + pv[HEAD_DIM:HEAD_DIM + 1]


    kb = key_chunk(first * MOBA_BLOCK)
    own_score = lambda c: _dot(kb[lanes[c][0]][:, head(lanes[c][1])], q_heads[c])
    score = own_score(0)
    cmax0, outs = [], []
    all_queries = jnp.ones((1, tq), _F32)
    for c, (sq, a) in enumerate(lanes):
        next_score = own_score(c + 1) if c + 1 < n_lanes else None
        _, l_c = fold_lane(c, prev_last_block, st_ref[CMAX, c], st_ref[RUN_MAX, c],
                           st_ref[RUN_SUM, c])
        outs.append(acc_ref[c] / l_c)
        acc_ref[c] = jnp.zeros((HEAD_DIM, tq), _F32)
        s = score + tab_ref[a, variant]
        s_ref[c] = s.astype(_BF16)
        flags = set_flags(c, [adj_flag[c]] + [all_queries] * (BLOCKS_PER_CHUNK - 1))
        cmax0.append(chunk_max(s, flags))
        score = next_score
    for sq in range(SEQS_PER_STEP):
        out_t = jnp.concatenate(outs[sq * N_HEADS:(sq + 1) * N_HEADS], axis=0)
        o_ref[sq] = out_t.T.astype(_BF16)

    def step(i, carry):
        cmax, m, l = carry
        kb = key_chunk(i * KEY_CHUNK)
        score_lane = lambda c: _dot(kb[lanes[c][0]][:, head(lanes[c][1])], q_heads[c])
        first_block = jnp.where(i == 0, first, (i - 1) * BLOCKS_PER_CHUNK)
        new_cmax, new_m, new_l = [], [], []
        score = score_lane(0)
        for c in range(n_lanes):
            next_score = score_lane(c + 1) if c + 1 < n_lanes else None
            m_c, l_c = fold_lane(c, first_block, cmax[c], m[c], l[c])
            s_ref[c] = score.astype(_BF16)
            flags = set_flags(c, [fsel_ref[c, pl.ds(i * BLOCKS_PER_CHUNK + r, 1), :]
                                  for r in range(BLOCKS_PER_CHUNK)])
            new_cmax.append(chunk_max(score, flags))
            new_m.append(m_c)
            new_l.append(l_c)
            score = next_score
        return tuple(new_cmax), tuple(new_m), tuple(new_l)

    init_m = tuple(jnp.full((1, tq), MASKED, _F32) for _ in range(n_lanes))
    init_l = tuple(jnp.zeros((1, tq), _F32) for _ in range(n_lanes))
    cmax, m, l = lax.fori_loop(0, n_far_chunks, step, (tuple(cmax0), init_m, init_l))
    for c in range(n_lanes):
        st_ref[CMAX, c] = cmax[c]
        st_ref[RUN_MAX, c] = m[c]
        st_ref[RUN_SUM, c] = l[c]


def _moba(qv3, k, tables, *, batch, seq_len):
    nb = seq_len // MOBA_BLOCK
    sq = SEQS_PER_STEP
    n_lanes = sq * N_HEADS
    assert nb % BLOCKS_PER_CHUNK == 0 and batch % sq == 0
    qv4 = qv3.reshape(batch, nb, 2 * D_ATTN, MOBA_BLOCK)
    k3 = k.reshape(batch, seq_len, D_ATTN)
    out = pl.pallas_call(
        _moba_kernel,
        grid=(batch // sq, nb + 1),
        in_specs=[
            pl.BlockSpec((sq, 1, D_ATTN, MOBA_BLOCK),
                         lambda g, t: (g, jnp.minimum(t, nb - 1), 0, 0)),
            pl.BlockSpec((sq, seq_len, D_ATTN), lambda g, t: (g, 0, 0)),
            pl.BlockSpec((sq, nb, D_ATTN, MOBA_BLOCK), lambda g, t: (g, 0, 1, 0)),
            pl.BlockSpec((N_HEADS, 2, KEY_CHUNK, MOBA_BLOCK), lambda g, t: (0, 0, 0, 0),
                         pipeline_mode=pl.Buffered(1)),
        ],
        out_specs=pl.BlockSpec((sq, MOBA_BLOCK, D_ATTN),
                               lambda g, t: (g, jnp.maximum(t - 1, 0), 0)),
        out_shape=jax.ShapeDtypeStruct((batch, seq_len, D_ATTN), _BF16),
        scratch_shapes=[pltpu.VMEM((sq, nb, D_ATTN), _F32),
                        pltpu.VMEM((n_lanes, nb, MOBA_BLOCK), _F32),
                        pltpu.VMEM((n_lanes, KEY_CHUNK, MOBA_BLOCK), _BF16),
                        pltpu.VMEM((n_lanes, HEAD_DIM, MOBA_BLOCK), _F32),
                        pltpu.VMEM((3 + BLOCKS_PER_CHUNK, n_lanes, 1, MOBA_BLOCK), _F32)],
        compiler_params=pltpu.CompilerParams(
            dimension_semantics=("arbitrary", "arbitrary"),
            vmem_limit_bytes=VMEM_LIMIT),
        name="moba_attention",
    )(qv4, k3, qv4, tables)
    return out.reshape(batch * seq_len, D_ATTN)


def _shift_rows(g, prev_rows, lag):
    rolled = pltpu.roll(g, lag, 0)
    rid = lax.broadcasted_iota(jnp.int32, prev_rows.shape, 0)
    head = jnp.where(rid < lag, pltpu.roll(prev_rows, lag, 0), rolled[:CONV_HALO])
    return jnp.concatenate([head, rolled[CONV_HALO:]], axis=0)


def _ffn_kernel(x_ref, p_ref, ypool_ref, yattn_ref, wo_ref, g_post_ref,
                g_pre_ref, wup_ref, convw_ref, convb_ref, wdown_ref, g_ffn_post_ref,
                wple_ref, wplegate_ref, o_ref, carry_ref, act_ref, *, tiles_per_seq):
    i = pl.program_id(0)
    tm = x_ref.shape[0]

    @pl.when(i % tiles_per_seq == 0)
    def _():
        carry_ref[...] = jnp.zeros(carry_ref.shape, _F32)

    y = _dot(ypool_ref[...], wo_ref[0:D_POOL, :]) + _dot(yattn_ref[...], wo_ref[D_POOL:, :])
    ple = _dot(p_ref[...].astype(_BF16), wple_ref[...])
    x1 = x_ref[...] + _rms(y, g_post_ref[...])
    h = _rms(x1, g_pre_ref[...]).astype(_BF16)

    for c in range(N_FF_CHUNKS):
        cols = slice(c * FF_CHUNK, (c + 1) * FF_CHUNK)
        gate = _dot(h, wup_ref[:, cols])
        val = _dot(h, wup_ref[:, D_FF + c * FF_CHUNK:D_FF + (c + 1) * FF_CHUNK])
        prev = carry_ref[:, cols]
        conv = convb_ref[:, cols] + _shift_rows(gate, prev, 2) * convw_ref[0:1, cols]
        conv = conv + _shift_rows(gate, prev, 1) * convw_ref[1:2, cols]
        conv = conv + gate * convw_ref[2:3, cols]
        carry_ref[:, cols] = gate[tm - CONV_HALO:, :]
        act_ref[:, cols] = (jax.nn.gelu(conv, approximate=True) * val).astype(_BF16)

    x2 = x1 + _rms(_dot(act_ref[...], wdown_ref[...]), g_ffn_post_ref[...])
    gate_logit = _dot(x2.astype(_BF16), wplegate_ref[...])
    o_ref[...] = x2 + ple * (1.0 / (1.0 + jnp.exp(-gate_logit)))


def _ffn(x2d, p3d, ypool, yattn, wo, g_post, g_pre, wup, convw, convb,
         wdown, g_ffn_post, wple, wplegate, *, layer, seq_len):
    t = x2d.shape[0]
    tm = ROW_TILE
    rows = lambda width: pl.BlockSpec((tm, width), lambda i: (i, 0))
    const = functools.partial(_layer_block, layer, pipeline_mode=pl.Buffered(1))
    return pl.pallas_call(
        functools.partial(_ffn_kernel, tiles_per_seq=seq_len // tm),
        grid=(t // tm,),
        in_specs=[
            rows(D_MODEL), pl.BlockSpec((None, tm, PLE_DIM), lambda i: (layer, i, 0)),
            rows(D_POOL), rows(D_ATTN),
            const((D_MODEL, D_MODEL)), const((1, D_MODEL)),
            const((1, D_MODEL)),
            const((D_MODEL, 2 * D_FF)),
            const((CONV_WIDTH, D_FF)), const((1, D_FF)),
            const((D_FF, D_MODEL)), const((1, D_MODEL)),
            const((PLE_DIM, D_MODEL)), const((D_MODEL, D_MODEL)),
        ],
        out_specs=rows(D_MODEL),
        out_shape=jax.ShapeDtypeStruct((t, D_MODEL), _F32),
        scratch_shapes=[pltpu.VMEM((CONV_HALO, D_FF), _F32), pltpu.VMEM((tm, D_FF), _BF16)],
        compiler_params=pltpu.CompilerParams(
            dimension_semantics=("arbitrary",), vmem_limit_bytes=VMEM_LIMIT),
        name="ffn_block",
    )(x2d, p3d, ypool, yattn, wo, g_post, g_pre, wup, convw, convb,
      wdown, g_ffn_post, wple, wplegate)


def kernel(x, p, rel_bias, g_mix_pre, g_mix_post, g_ffn_pre, g_ffn_post, w_in, w_pool,
           pool_scale, w_out, w_up, conv_w, conv_b, w_down, w_ple, w_ple_gate):
    batch, seq_len, d_model = x.shape
    depth = w_in.shape[0]
    assert d_model == D_MODEL and seq_len % ROW_TILE == 0
    assert seq_len % PROJ_ROW_TILE == 0 and PROJ_ROW_TILE % MOBA_BLOCK == 0
    t = batch * seq_len
    rows = lambda v: v.reshape(depth, 1, -1).astype(_F32)
    bf16 = lambda w: w.astype(_BF16)
    w_in_b, w_pool_b, w_out_b, w_up_b, w_down_b = map(bf16, (w_in, w_pool, w_out, w_up, w_down))
    w_ple_b, w_ple_gate_b = bf16(w_ple), bf16(w_ple_gate)
    p3d = p.reshape(depth, t, PLE_DIM).astype(_F32)

    tables = _bias_tables(rel_bias)
    xf = x.reshape(t, d_model).astype(_F32)
    for i in range(depth):
        k, qv3, ypool = _proj_pool(
            xf, rows(g_mix_pre), w_in_b, w_pool_b, rows(pool_scale),
            layer=i, seq_len=seq_len)
        yattn = _moba(qv3, k, tables, batch=batch, seq_len=seq_len)
        xf = _ffn(
            xf, p3d, ypool, yattn, w_out_b, rows(g_mix_post), rows(g_ffn_pre), w_up_b,
            conv_w.astype(_F32), rows(conv_b), w_down_b, rows(g_ffn_post), w_ple_b,
            w_ple_gate_b, layer=i, seq_len=seq_len)
    return xf.reshape(batch, seq_len, d_model).astype(x.dtype)
```

```python
import functools
import math

import numpy as np
import jax
import jax.numpy as jnp
from jax import lax
from jax.experimental import pallas as pl
from jax.experimental.pallas import tpu as pltpu

D_MODEL = 1024
PLE_DIM = 256
D_POOL = 512
POOL_WINDOWS = (2, 4, 8, 16)
POOL_GROUP = D_POOL // len(POOL_WINDOWS)
D_ATTN = D_MODEL - D_POOL
N_HEADS = 8
HEAD_DIM = D_ATTN // N_HEADS
MOBA_BLOCK = 256
MOBA_TOPK = 3
NUM_BUCKETS = 32
MAX_DISTANCE = 128
D_FF = 2816
CONV_WIDTH = 3
EPS = 1e-6

SEQS_PER_STEP = 2
POOL_HIST = 16
POOL_HALO = 2 * POOL_HIST
CONV_HALO = 8
FF_CHUNK = 256
N_FF_CHUNKS = D_FF // FF_CHUNK
ROW_TILE = 1024
PROJ_ROW_TILE = 1024
MASKED = -1e30
LOG2_E = math.log2(math.e)
BLOCKS_PER_CHUNK = 2
KEY_CHUNK = BLOCKS_PER_CHUNK * MOBA_BLOCK
SUM_ROWS = 16
VMEM_LIMIT = 56 * 1024 * 1024
FFN_VMEM_LIMIT = 62 * 1024 * 1024

_BF16 = jnp.bfloat16
_F32 = jnp.float32


def _rms(x, g):
    return x * lax.rsqrt(jnp.mean(x * x, axis=-1, keepdims=True) + EPS) * g


def _dot(a, b):
    return jnp.dot(a, b, preferred_element_type=_F32)


def _bucket_tables():
    key = np.arange(MOBA_BLOCK)[:, None]
    qry = np.arange(MOBA_BLOCK)[None, :]

    def bucket(dist):
        n = np.maximum(dist, 0)
        max_exact = NUM_BUCKETS // 2
        nf = np.maximum(n, 1).astype(np.float32)
        large = max_exact + (np.log(nf / max_exact) / math.log(MAX_DISTANCE / max_exact)
                             * (NUM_BUCKETS - max_exact)).astype(np.int32)
        large = np.minimum(large, NUM_BUCKETS - 1)
        return np.where(n < max_exact, n, large).astype(np.int32)

    d_own = qry - key
    own = np.where(d_own >= 0, bucket(d_own), -1).astype(np.int32)
    adj = bucket(d_own + MOBA_BLOCK)
    return own, adj


def _bias_table_kernel(rb_ref, own_idx_ref, adj_idx_ref, tab_ref):
    h = pl.program_id(0)
    far = rb_ref[NUM_BUCKETS - 1, h]
    own_idx = own_idx_ref[...]
    adj_idx = adj_idx_ref[...]
    own = jnp.zeros(own_idx.shape, _F32)
    adj = jnp.zeros(adj_idx.shape, _F32)
    for b in range(NUM_BUCKETS):
        val = (rb_ref[b, h] - far) * LOG2_E
        own = jnp.where(own_idx == b, val, own)
        adj = jnp.where(adj_idx == b, val, adj)
    own = jnp.where(own_idx < 0, MASKED, own)
    tab_ref[0, 0, 0:MOBA_BLOCK, :] = adj
    tab_ref[0, 0, MOBA_BLOCK:, :] = own
    tab_ref[0, 1, 0:MOBA_BLOCK, :] = own
    tab_ref[0, 1, MOBA_BLOCK:, :] = jnp.full(own.shape, MASKED, _F32)


def _bias_tables(rel_bias):
    own_idx, adj_idx = _bucket_tables()
    blk = pl.BlockSpec((MOBA_BLOCK, MOBA_BLOCK), lambda h: (0, 0))
    return pl.pallas_call(
        _bias_table_kernel,
        grid=(N_HEADS,),
        in_specs=[pl.BlockSpec(memory_space=pltpu.SMEM), blk, blk],
        out_specs=pl.BlockSpec((1, 2, KEY_CHUNK, MOBA_BLOCK), lambda h: (h, 0, 0, 0)),
        out_shape=jax.ShapeDtypeStruct((N_HEADS, 2, KEY_CHUNK, MOBA_BLOCK), _F32),
        name="bias_tables",
    )(rel_bias.astype(_F32), jnp.asarray(own_idx), jnp.asarray(adj_idx))


def _proj_pool_kernel(x_ref, g_ref, win_ref, wpool_ref, pscale_ref,
                      k_ref, qv_ref, ypool_ref, ext_ref, lvl_ref, *, tiles_per_seq):
    i = pl.program_id(0)
    tm = x_ref.shape[0]
    h = _rms(x_ref[...], g_ref[...]).astype(_BF16)
    seq_tile = i % tiles_per_seq

    @pl.when(seq_tile == 0)
    def _():
        ext_ref[0:POOL_HALO, :] = jnp.zeros((POOL_HALO, D_POOL), _F32)

    u = _dot(h, win_ref[:, 0:D_POOL])
    ext_ref[POOL_HALO:POOL_HALO + tm, :] = u

    k_ref[...] = _dot(h, win_ref[:, D_POOL + D_ATTN:D_POOL + 2 * D_ATTN]).astype(_BF16)

    q_t = (_dot(h, win_ref[:, D_POOL:D_POOL + D_ATTN]) * (HEAD_DIM ** -0.5 * LOG2_E)).T
    v_t = _dot(h, win_ref[:, D_POOL + 2 * D_ATTN:]).T
    for c in range(tm // MOBA_BLOCK):
        cols = slice(c * MOBA_BLOCK, (c + 1) * MOBA_BLOCK)
        qv_ref[c, 0:D_ATTN, :] = q_t[:, cols].astype(_BF16)
        qv_ref[c, D_ATTN:, :] = v_t[:, cols].astype(_BF16)

    end = POOL_HALO + tm
    pos = seq_tile * tm + lax.broadcasted_iota(jnp.int32, (tm, 1), 0)
    for g, w in enumerate(POOL_WINDOWS):
        cols = slice(g * POOL_GROUP, (g + 1) * POOL_GROUP)
        levels = w.bit_length() - 1
        assert w == 1 << levels and 1 <= levels <= 4
        read = lambda lo, hi: ext_ref[lo:hi, cols]
        for lvl in range(1, levels + 1):
            delay = 1 << (lvl - 1)
            lo = POOL_HALO if lvl == levels else 8 * lvl
            acc = read(lo, end) + read(lo - delay, end - delay)
            if lvl < levels:
                slot = lvl % 2
                lvl_ref[slot, lo:end, :] = acc
                read = lambda lo, hi, slot=slot: lvl_ref[slot, lo:hi, :]
        u_g = u[:, cols]
        count = jnp.minimum(pos + 1, w).astype(_F32)
        pooled = (acc / count - u_g).astype(_BF16)
        mixed = _dot(pooled, wpool_ref[g]) * pscale_ref[:, cols]
        ypool_ref[:, cols] = mixed.astype(_BF16)
    ext_ref[POOL_HALO - POOL_HIST:POOL_HALO, :] = u[tm - POOL_HIST:, :]


def _layer_block(layer, shape, **kwargs):
    return pl.BlockSpec((None,) + shape, lambda i: (layer,) + (0,) * len(shape), **kwargs)


def _proj_pool(x2d, g, win, wpool, pscale, *, layer, seq_len):
    t = x2d.shape[0]
    tm = PROJ_ROW_TILE
    const = functools.partial(_layer_block, layer)
    return pl.pallas_call(
        functools.partial(_proj_pool_kernel, tiles_per_seq=seq_len // tm),
        grid=(t // tm,),
        in_specs=[
            pl.BlockSpec((tm, D_MODEL), lambda i: (i, 0)),
            const((1, D_MODEL)),
            const((D_MODEL, D_POOL + 3 * D_ATTN)),
            const((len(POOL_WINDOWS), POOL_GROUP, POOL_GROUP)),
            const((1, D_POOL)),
        ],
        out_specs=[
            pl.BlockSpec((tm, D_ATTN), lambda i: (i, 0)),
            pl.BlockSpec((tm // MOBA_BLOCK, 2 * D_ATTN, MOBA_BLOCK), lambda i: (i, 0, 0)),
            pl.BlockSpec((tm, D_POOL), lambda i: (i, 0)),
        ],
        out_shape=[
            jax.ShapeDtypeStruct((t, D_ATTN), _BF16),
            jax.ShapeDtypeStruct((t // MOBA_BLOCK, 2 * D_ATTN, MOBA_BLOCK), _BF16),
            jax.ShapeDtypeStruct((t, D_POOL), _BF16),
        ],
        scratch_shapes=[pltpu.VMEM((POOL_HALO + tm, D_POOL), _F32),
                        pltpu.VMEM((2, POOL_HALO + tm, POOL_GROUP), _F32)],
        compiler_params=pltpu.CompilerParams(
            dimension_semantics=("arbitrary",), vmem_limit_bytes=VMEM_LIMIT),
        name="proj_pool",
    )(x2d, g, win, wpool, pscale)


def _select_blocks(gate, n_past):
    blk = lax.broadcasted_iota(jnp.int32, gate.shape, 0)
    n_blocks = gate.shape[0]
    g = jnp.where(blk < n_past, gate, -jnp.inf)
    sel = jnp.zeros(gate.shape, _F32)
    for _ in range(MOBA_TOPK):
        mx = jnp.max(g, axis=0, keepdims=True)
        is_max = jnp.logical_and(g == mx, mx > -jnp.inf)
        first = jnp.min(jnp.where(is_max, blk, n_blocks), axis=0, keepdims=True)
        pick = blk == first
        sel = jnp.where(pick, 1.0, sel)
        g = jnp.where(pick, -jnp.inf, g)
    return sel


def _moba_kernel(q_ref, k_ref, v_ref, tab_ref, o_ref, kmean_ref, fsel_ref, s_ref, acc_ref,
                 st_ref):
    t = pl.program_id(1)
    nb = v_ref.shape[1]
    tq = q_ref.shape[3]
    lanes = [(sq, a) for sq in range(SEQS_PER_STEP) for a in range(N_HEADS)]
    n_lanes = len(lanes)
    head = lambda a: slice(a * HEAD_DIM, (a + 1) * HEAD_DIM)
    CMAX, RUN_MAX, RUN_SUM, FLAGS = 0, 1, 2, 3

    @pl.when(t == 0)
    def _():
        for sq in range(SEQS_PER_STEP):
            for n in range(nb):
                kb = k_ref[sq, n * MOBA_BLOCK:(n + 1) * MOBA_BLOCK, :].astype(_F32)
                kmean_ref[sq, n:n + 1, :] = (
                    jnp.sum(kb, axis=0, keepdims=True) * (1.0 / MOBA_BLOCK))
        s_ref[...] = jnp.full(s_ref.shape, MASKED, _F32)
        acc_ref[...] = jnp.zeros(acc_ref.shape, _F32)
        st_ref[CMAX] = jnp.full(st_ref.shape[1:], MASKED, _F32)
        st_ref[RUN_MAX] = jnp.full(st_ref.shape[1:], MASKED, _F32)
        st_ref[RUN_SUM] = jnp.ones(st_ref.shape[1:], _F32)
        for r in range(BLOCKS_PER_CHUNK):
            st_ref[FLAGS + r] = jnp.zeros(st_ref.shape[1:], _F32)

    j = jnp.minimum(t, nb - 1)
    prev = jnp.maximum(t - 1, 0)
    q_heads = [q_ref[sq, 0, head(a), :] for sq, a in lanes]

    kmean = [kmean_ref[sq].astype(_BF16) for sq in range(SEQS_PER_STEP)]
    blk = lax.broadcasted_iota(jnp.int32, (nb, tq), 0)
    adj_flag = []
    for c, (sq, a) in enumerate(lanes):
        sel = _select_blocks(_dot(kmean[sq][:, head(a)], q_heads[c]), j)
        is_adj = blk == j - 1
        fsel_ref[c] = jnp.where(is_adj, 0.0, sel)
        flag = jnp.max(jnp.where(is_adj, sel, 0.0), axis=0, keepdims=True)
        adj_flag.append(jnp.where(j == 0, 1.0, flag))

    first = jnp.maximum(j - 1, 0)
    variant = jnp.where(j == 0, 1, 0)
    n_far_chunks = jnp.where(t < nb, lax.shift_right_logical(j, 1), 0)
    prev_far_chunks = lax.shift_right_logical(prev, 1)
    prev_last_block = jnp.where(prev_far_chunks == 0, jnp.maximum(prev - 1, 0),
                                (prev_far_chunks - 1) * BLOCKS_PER_CHUNK)

    def key_chunk(first_row):
        rows = pl.ds(pl.multiple_of(first_row, MOBA_BLOCK), KEY_CHUNK)
        return [k_ref[sq, rows, :] for sq in range(SEQS_PER_STEP)]

    ones = jnp.ones((SUM_ROWS, MOBA_BLOCK), _BF16)

    def block_rows(r):
        return slice(r * MOBA_BLOCK, (r + 1) * MOBA_BLOCK)

    def chunk_max(s, flags):
        cmax = None
        for r in range(BLOCKS_PER_CHUNK):
            bmax = jnp.max(s[block_rows(r)], axis=0, keepdims=True)
            bmax = jnp.where(flags[r] > 0.0, bmax, MASKED)
            cmax = bmax if cmax is None else jnp.maximum(cmax, bmax)
        return cmax

    def set_flags(c, flags):
        for r in range(BLOCKS_PER_CHUNK):
            st_ref[FLAGS + r, c] = flags[r]
        return [st_ref[FLAGS + r, c] for r in range(BLOCKS_PER_CHUNK)]

    def fold_lane(c, first_block, cmax, m, l):
        sq, a = lanes[c]
        m_new = jnp.maximum(m, cmax)
        alpha = jnp.exp2(m - m_new)
        pv = None
        for r in range(BLOCKS_PER_CHUNK):
            offset = jnp.where(st_ref[FLAGS + r, c] > 0.0, -m_new, MASKED)
            p = jnp.exp2(s_ref[c, block_rows(r), :] + offset).astype(_BF16)
            v_t = jnp.concatenate([v_ref[sq, first_block + r, head(a), :], ones], axis=0)
            d = _dot(v_t, p)
            pv = d if pv is None else pv + d
        acc_ref[c] = alpha * acc_ref[c] + pv[:HEAD_DIM]
        return m_new, alpha * l + pv[HEAD_DIM:HEAD_DIM + 1]


    kb = key_chunk(first * MOBA_BLOCK)
    own_score = lambda c: _dot(kb[lanes[c][0]][:, head(lanes[c][1])], q_heads[c])
    score = own_score(0)
    cmax0, outs = [], []
    all_queries = jnp.ones((1, tq), _F32)
    for c, (sq, a) in enumerate(lanes):
        next_score = own_score(c + 1) if c + 1 < n_lanes else None
        _, l_c = fold_lane(c, prev_last_block, st_ref[CMAX, c], st_ref[RUN_MAX, c],
                           st_ref[RUN_SUM, c])
        outs.append(acc_ref[c] / l_c)
        acc_ref[c] = jnp.zeros((HEAD_DIM, tq), _F32)
        s = score + tab_ref[a, variant]
        s_ref[c] = s
        flags = set_flags(c, [adj_flag[c]] + [all_queries] * (BLOCKS_PER_CHUNK - 1))
        cmax0.append(chunk_max(s, flags))
        score = next_score
    for sq in range(SEQS_PER_STEP):
        out_t = jnp.concatenate(outs[sq * N_HEADS:(sq + 1) * N_HEADS], axis=0)
        o_ref[sq] = out_t.T.astype(_BF16)

    def step(i, carry):
        cmax, m, l = carry
        kb = key_chunk(i * KEY_CHUNK)
        score_lane = lambda c: _dot(kb[lanes[c][0]][:, head(lanes[c][1])], q_heads[c])
        first_block = jnp.where(i == 0, first, (i - 1) * BLOCKS_PER_CHUNK)
        new_cmax, new_m, new_l = [], [], []
        score = score_lane(0)
        for c in range(n_lanes):
            next_score = score_lane(c + 1) if c + 1 < n_lanes else None
            m_c, l_c = fold_lane(c, first_block, cmax[c], m[c], l[c])
            s_ref[c] = score
            flags = set_flags(c, [fsel_ref[c, pl.ds(i * BLOCKS_PER_CHUNK + r, 1), :]
                                  for r in range(BLOCKS_PER_CHUNK)])
            new_cmax.append(chunk_max(score, flags))
            new_m.append(m_c)
            new_l.append(l_c)
            score = next_score
        return tuple(new_cmax), tuple(new_m), tuple(new_l)

    init_m = tuple(jnp.full((1, tq), MASKED, _F32) for _ in range(n_lanes))
    init_l = tuple(jnp.zeros((1, tq), _F32) for _ in range(n_lanes))
    cmax, m, l = lax.fori_loop(0, n_far_chunks, step, (tuple(cmax0), init_m, init_l))
    for c in range(n_lanes):
        st_ref[CMAX, c] = cmax[c]
        st_ref[RUN_MAX, c] = m[c]
        st_ref[RUN_SUM, c] = l[c]


def _moba(qv3, k, tables, *, batch, seq_len):
    nb = seq_len // MOBA_BLOCK
    sq = SEQS_PER_STEP
    n_lanes = sq * N_HEADS
    assert nb % BLOCKS_PER_CHUNK == 0 and batch % sq == 0
    qv4 = qv3.reshape(batch, nb, 2 * D_ATTN, MOBA_BLOCK)
    k3 = k.reshape(batch, seq_len, D_ATTN)
    out = pl.pallas_call(
        _moba_kernel,
        grid=(batch // sq, nb + 1),
        in_specs=[
            pl.BlockSpec((sq, 1, D_ATTN, MOBA_BLOCK),
                         lambda g, t: (g, jnp.minimum(t, nb - 1), 0, 0)),
            pl.BlockSpec((sq, seq_len, D_ATTN), lambda g, t: (g, 0, 0)),
            pl.BlockSpec((sq, nb, D_ATTN, MOBA_BLOCK), lambda g, t: (g, 0, 1, 0)),
            pl.BlockSpec((N_HEADS, 2, KEY_CHUNK, MOBA_BLOCK), lambda g, t: (0, 0, 0, 0),
                         pipeline_mode=pl.Buffered(1)),
        ],
        out_specs=pl.BlockSpec((sq, MOBA_BLOCK, D_ATTN),
                               lambda g, t: (g, jnp.maximum(t - 1, 0), 0)),
        out_shape=jax.ShapeDtypeStruct((batch, seq_len, D_ATTN), _BF16),
        scratch_shapes=[pltpu.VMEM((sq, nb, D_ATTN), _F32),
                        pltpu.VMEM((n_lanes, nb, MOBA_BLOCK), _F32),
                        pltpu.VMEM((n_lanes, KEY_CHUNK, MOBA_BLOCK), _F32),
                        pltpu.VMEM((n_lanes, HEAD_DIM, MOBA_BLOCK), _F32),
                        pltpu.VMEM((3 + BLOCKS_PER_CHUNK, n_lanes, 1, MOBA_BLOCK), _F32)],
        compiler_params=pltpu.CompilerParams(
            dimension_semantics=("arbitrary", "arbitrary"),
            vmem_limit_bytes=VMEM_LIMIT),
        name="moba_attention",
    )(qv4, k3, qv4, tables)
    return out.reshape(batch * seq_len, D_ATTN)


def _shift_rows(g, prev_rows, lag):
    rolled = pltpu.roll(g, lag, 0)
    rid = lax.broadcasted_iota(jnp.int32, prev_rows.shape, 0)
    head = jnp.where(rid < lag, pltpu.roll(prev_rows, lag, 0), rolled[:CONV_HALO])
    return jnp.concatenate([head, rolled[CONV_HALO:]], axis=0)


def _ffn_kernel(x_ref, p_ref, ypool_ref, yattn_ref, wo_ref, g_post_ref,
                g_pre_ref, wup_ref, convw_ref, convb_ref, wdown_ref, g_ffn_post_ref,
                wple_ref, wplegate_ref, o_ref, carry_ref, act_ref, *, tiles_per_seq):
    i = pl.program_id(0)
    tm = x_ref.shape[0]

    @pl.when(i % tiles_per_seq == 0)
    def _():
        carry_ref[...] = jnp.zeros(carry_ref.shape, _F32)

    y = _dot(ypool_ref[...], wo_ref[0:D_POOL, :]) + _dot(yattn_ref[...], wo_ref[D_POOL:, :])
    ple = _dot(p_ref[...].astype(_BF16), wple_ref[...])
    x1 = x_ref[...] + _rms(y, g_post_ref[...])
    h = _rms(x1, g_pre_ref[...]).astype(_BF16)

    for c in range(N_FF_CHUNKS):
        cols = slice(c * FF_CHUNK, (c + 1) * FF_CHUNK)
        gate = _dot(h, wup_ref[:, cols])
        val = _dot(h, wup_ref[:, D_FF + c * FF_CHUNK:D_FF + (c + 1) * FF_CHUNK])
        prev = carry_ref[:, cols]
        conv = convb_ref[:, cols] + _shift_rows(gate, prev, 2) * convw_ref[0:1, cols]
        conv = conv + _shift_rows(gate, prev, 1) * convw_ref[1:2, cols]
        conv = conv + gate * convw_ref[2:3, cols]
        carry_ref[:, cols] = gate[tm - CONV_HALO:, :]
        act_ref[:, cols] = (jax.nn.gelu(conv, approximate=True) * val).astype(_BF16)

    x2 = x1 + _rms(_dot(act_ref[...], wdown_ref[...]), g_ffn_post_ref[...])
    gate_logit = _dot(x2.astype(_BF16), wplegate_ref[...])
    o_ref[...] = x2 + ple * (1.0 / (1.0 + jnp.exp(-gate_logit)))


def _ffn(x2d, p3d, ypool, yattn, wo, g_post, g_pre, wup, convw, convb,
         wdown, g_ffn_post, wple, wplegate, *, layer, seq_len):
    t = x2d.shape[0]
    tm = ROW_TILE
    rows = lambda width: pl.BlockSpec((tm, width), lambda i: (i, 0))
    const = functools.partial(_layer_block, layer, pipeline_mode=pl.Buffered(1))
    return pl.pallas_call(
        functools.partial(_ffn_kernel, tiles_per_seq=seq_len // tm),
        grid=(t // tm,),
        in_specs=[
            rows(D_MODEL), pl.BlockSpec((None, tm, PLE_DIM), lambda i: (layer, i, 0)),
            rows(D_POOL), rows(D_ATTN),
            const((D_MODEL, D_MODEL)), const((1, D_MODEL)),
            const((1, D_MODEL)),
            const((D_MODEL, 2 * D_FF)),
            const((CONV_WIDTH, D_FF)), const((1, D_FF)),
            const((D_FF, D_MODEL)), const((1, D_MODEL)),
            const((PLE_DIM, D_MODEL)), const((D_MODEL, D_MODEL)),
        ],
        out_specs=rows(D_MODEL),
        out_shape=jax.ShapeDtypeStruct((t, D_MODEL), _F32),
        scratch_shapes=[pltpu.VMEM((CONV_HALO, D_FF), _F32), pltpu.VMEM((tm, D_FF), _BF16)],
        compiler_params=pltpu.CompilerParams(
            dimension_semantics=("arbitrary",), vmem_limit_bytes=FFN_VMEM_LIMIT),
        name="ffn_block",
    )(x2d, p3d, ypool, yattn, wo, g_post, g_pre, wup, convw, convb,
      wdown, g_ffn_post, wple, wplegate)


def kernel(x, p, rel_bias, g_mix_pre, g_mix_post, g_ffn_pre, g_ffn_post, w_in, w_pool,
           pool_scale, w_out, w_up, conv_w, conv_b, w_down, w_ple, w_ple_gate):
    batch, seq_len, d_model = x.shape
    depth = w_in.shape[0]
    assert d_model == D_MODEL and seq_len % ROW_TILE == 0
    assert seq_len % PROJ_ROW_TILE == 0 and PROJ_ROW_TILE % MOBA_BLOCK == 0
    t = batch * seq_len
    rows = lambda v: v.reshape(depth, 1, -1).astype(_F32)
    bf16 = lambda w: w.astype(_BF16)
    w_in_b, w_pool_b, w_out_b, w_up_b, w_down_b = map(bf16, (w_in, w_pool, w_out, w_up, w_down))
    w_ple_b, w_ple_gate_b = bf16(w_ple), bf16(w_ple_gate)
    p3d = p.reshape(depth, t, PLE_DIM).astype(_F32)

    tables = _bias_tables(rel_bias)
    xf = x.reshape(t, d_model).astype(_F32)
    for i in range(depth):
        k, qv3, ypool = _proj_pool(
            xf, rows(g_mix_pre), w_in_b, w_pool_b, rows(pool_scale),
            layer=i, seq_len=seq_len)
        yattn = _moba(qv3, k, tables, batch=batch, seq_len=seq_len)
        xf = _ffn(
            xf, p3d, ypool, yattn, w_out_b, rows(g_mix_post), rows(g_ffn_pre), w_up_b,
            conv_w.astype(_F32), rows(conv_b), w_down_b, rows(g_ffn_post), w_ple_b,
            w_ple_gate_b, layer=i, seq_len=seq_len)
    return xf.reshape(batch, seq_len, d_model).astype(x.dtype)
```
